```python
import jax, jax.numpy as jnp
from jax import lax
import numpy as np

D_MODEL = 2048
BATCH = 2
SEQ = 8192
DEPTH = 2

N_MIXERS = 2
N_RET_LAYERS = (DEPTH + 1) // 2
N_CONV_LAYERS = DEPTH // 2

RET_HEADS = 8
RET_QK_DIM = D_MODEL // RET_HEADS
RET_V_DIM = 2 * RET_QK_DIM
RET_CHUNK = 128
ROPE_BASE = 10000.0

CONV_WIDTH = 31
CONV_CH = D_MODEL

N_EXPERTS = 64
TOP_K = 8
N_GROUPS = 8
TOPK_GROUPS = 4
EXPERT_FF = 512
SHARED_FF = 512
ROUTED_SCALE = 2.5
EXPERT_BLOCK = 256

NORM_EPS = 1e-6

kernel_name = "hybrid_retention_conformer_moe_adaln"


def rms_norm(x, g):
    xf = x.astype(jnp.float32)
    y = xf * lax.rsqrt(jnp.mean(xf * xf, axis=-1, keepdims=True) + NORM_EPS)
    return (y * g.astype(jnp.float32)).astype(x.dtype)


def modulate(h, shift, scale):
    return h * (1 + scale[:, None, :]) + shift[:, None, :]


def rotary(x, positions):
    B, S, H, d = x.shape
    half = d // 2
    inv = 1.0 / (ROPE_BASE ** jnp.linspace(0.0, 1.0, half, dtype=jnp.float32))
    ang = positions.astype(jnp.float32)[:, :, None] * inv
    cos = jnp.cos(ang)[:, :, None, :]
    sin = jnp.sin(ang)[:, :, None, :]
    xp = x.astype(jnp.float32).reshape(B, S, H, half, 2)
    x1, x2 = xp[..., 0], xp[..., 1]
    out = jnp.stack([x1 * cos - x2 * sin, x1 * sin + x2 * cos], axis=-1).reshape(B, S, H, d)
    return out.astype(x.dtype)


def retention(h, positions, w_in, w_out, out_gain):
    B, S, _ = h.shape
    H, dk, dv, C = RET_HEADS, RET_QK_DIM, RET_V_DIM, RET_CHUNK
    proj = h @ w_in
    q, k, v, g = jnp.split(proj, [H * dk, 2 * H * dk, 2 * H * dk + H * dv], axis=-1)
    q = rotary(q.reshape(B, S, H, dk), positions)
    k = rotary(k.reshape(B, S, H, dk), positions) * (dk ** -0.5)
    v = v.reshape(B, S, H, dv)
    nc = S // C

    def to_chunks(t):
        return t.astype(jnp.float32).reshape(B, nc, C, H, t.shape[-1]).transpose(1, 0, 3, 2, 4)

    log_gamma = jnp.log1p(-jnp.exp2(-5.0 - jnp.arange(H, dtype=jnp.float32)))
    idx = jnp.arange(C, dtype=jnp.float32)
    rel = idx[:, None] - idx[None, :]
    intra = jnp.where(rel >= 0, jnp.exp(log_gamma[:, None, None] * jnp.maximum(rel, 0.0)), 0.0)
    q_decay = jnp.exp(log_gamma[:, None] * (idx + 1.0))[:, :, None]
    k_decay = jnp.exp(log_gamma[:, None] * (C - 1.0 - idx))[:, :, None]
    chunk_decay = jnp.exp(log_gamma * C)[:, None, None]

    def step(state, qkv):
        qc, kc, vc = qkv
        scores = jnp.einsum('bhnd,bhmd->bhnm', qc, kc) * intra
        inner = jnp.einsum('bhnm,bhme->bhne', scores, vc)
        cross = jnp.einsum('bhnd,bhde->bhne', qc * q_decay, state)
        state = chunk_decay * state + jnp.einsum('bhmd,bhme->bhde', kc * k_decay, vc)
        return state, inner + cross

    s0 = jnp.zeros((B, H, dk, dv), jnp.float32)
    _, o = lax.scan(step, s0, (to_chunks(q), to_chunks(k), to_chunks(v)))
    o = o.transpose(1, 0, 3, 2, 4).reshape(B, S, H, dv)
    o = o * lax.rsqrt(jnp.mean(o * o, axis=-1, keepdims=True) + NORM_EPS)
    o = o.reshape(B, S, H * dv) * out_gain.astype(jnp.float32)
    y = (jax.nn.silu(g.astype(jnp.float32)) * o).astype(h.dtype)
    return y @ w_out


def conv_module(h, w_in, dw_w, dw_b, ln_g, ln_b, w_out):
    a, b = jnp.split(h @ w_in, 2, axis=-1)
    u = a * jax.nn.sigmoid(b)
    u = lax.conv_general_dilated(u, dw_w[:, None, :], window_strides=(1,),
                                 padding=[(CONV_WIDTH - 1, 0)],
                                 dimension_numbers=('NWC', 'WIO', 'NWC'),
                                 feature_group_count=CONV_CH) + dw_b
    uf = u.astype(jnp.float32)
    mu = jnp.mean(uf, axis=-1, keepdims=True)
    var = jnp.mean(jnp.square(uf - mu), axis=-1, keepdims=True)
    uf = (uf - mu) * lax.rsqrt(var + NORM_EPS) * ln_g.astype(jnp.float32) + ln_b.astype(jnp.float32)
    u = jax.nn.silu(uf).astype(h.dtype)
    return u @ w_out


def swiglu(t, w_gu, w_dn):
    g, u = jnp.split(t @ w_gu, 2, axis=-1)
    return (jax.nn.silu(g) * u) @ w_dn


def moe(h, router_w, router_bias, w_gu, w_dn, sw_gu, sw_dn):
    B, S, D = h.shape
    N = B * S
    t = h.reshape(N, D)
    scores = jax.nn.sigmoid(t.astype(jnp.float32) @ router_w.astype(jnp.float32))
    biased = scores + router_bias.astype(jnp.float32)
    grp_score = lax.top_k(biased.reshape(N, N_GROUPS, N_EXPERTS // N_GROUPS), 2)[0].sum(-1)
    _, gidx = lax.top_k(grp_score, TOPK_GROUPS)
    gmask = jax.nn.one_hot(gidx, N_GROUPS).sum(-2) > 0
    emask = jnp.repeat(gmask, N_EXPERTS // N_GROUPS, axis=-1)
    _, eidx = lax.top_k(jnp.where(emask, biased, -jnp.inf), TOP_K)
    wts = jnp.take_along_axis(scores, eidx, axis=-1)
    wts = wts / jnp.sum(wts, axis=-1, keepdims=True) * ROUTED_SCALE

    BLK = EXPERT_BLOCK
    A = N * TOP_K
    flat_e = eidx.reshape(A)
    flat_tok = jnp.repeat(jnp.arange(N, dtype=jnp.int32), TOP_K)
    flat_w = wts.reshape(A)
    order = jnp.argsort(flat_e)
    se = flat_e[order]
    counts = jnp.bincount(flat_e, length=N_EXPERTS)
    starts = jnp.cumsum(counts) - counts
    padded = (counts + BLK - 1) // BLK * BLK
    pends = jnp.cumsum(padded)
    pstarts = pends - padded
    dest = pstarts[se] + (jnp.arange(A, dtype=jnp.int32) - starts[se])
    n_blocks = -(-A // BLK) + N_EXPERTS
    P = n_blocks * BLK
    row_tok = jnp.full((P,), N, jnp.int32).at[dest].set(flat_tok[order])
    row_w = jnp.zeros((P,), jnp.float32).at[dest].set(flat_w[order])
    block_e = jnp.minimum(jnp.searchsorted(pends, jnp.arange(n_blocks) * BLK, side='right'),
                          N_EXPERTS - 1)
    t_pad = jnp.concatenate([t, jnp.zeros((1, D), t.dtype)], axis=0)

    def body(acc, blk):
        tok, w, e = blk
        y = swiglu(t_pad[tok], w_gu[e], w_dn[e])
        return acc.at[tok].add(y.astype(jnp.float32) * w[:, None]), None

    acc, _ = lax.scan(body, jnp.zeros((N + 1, D), jnp.float32),
                      (row_tok.reshape(n_blocks, BLK), row_w.reshape(n_blocks, BLK), block_e))
    out = acc[:N] + swiglu(t, sw_gu, sw_dn).astype(jnp.float32)
    return out.astype(h.dtype).reshape(B, S, D)


def setup_inputs(seed: int = 0) -> dict:
    key = jax.random.key(seed)
    ks = jax.random.split(key, 24)
    f32 = jnp.float32
    D, H, dk, dv = D_MODEL, RET_HEADS, RET_QK_DIM, RET_V_DIM
    nrm = lambda k, shape, s: jax.random.normal(k, shape, f32) * s
    ret_in_cols = 2 * H * dk + 2 * H * dv
    start = jax.random.randint(ks[2], (BATCH,), 0, 4096, dtype=jnp.int32)
    return {
        "x": nrm(ks[0], (BATCH, SEQ, D), 1.0),
        "c": nrm(ks[1], (BATCH, D), 1.0),
        "positions": start[:, None] + jnp.arange(SEQ, dtype=jnp.int32)[None, :],
        "mod_w": nrm(ks[3], (DEPTH, D, 6 * D), 0.5 * D ** -0.5),
        "mod_b": nrm(ks[4], (DEPTH, 6 * D), 0.02),
        "norm_mix": 1.0 + nrm(ks[5], (DEPTH, D), 0.02),
        "norm_ffn": 1.0 + nrm(ks[6], (DEPTH, D), 0.02),
        "ret_w_in": nrm(ks[7], (N_RET_LAYERS, D, ret_in_cols), D ** -0.5),
        "ret_w_out": nrm(ks[8], (N_RET_LAYERS, H * dv, D), (H * dv) ** -0.5),
        "ret_out_gain": 1.0 + nrm(ks[9], (N_RET_LAYERS, H * dv), 0.02),
        "conv_w_in": nrm(ks[10], (N_CONV_LAYERS, D, 2 * CONV_CH), D ** -0.5),
        "conv_dw_w": nrm(ks[11], (N_CONV_LAYERS, CONV_WIDTH, CONV_CH), CONV_WIDTH ** -0.5),
        "conv_dw_b": nrm(ks[12], (N_CONV_LAYERS, CONV_CH), 0.02),
        "conv_ln_g": 1.0 + nrm(ks[13], (N_CONV_LAYERS, CONV_CH), 0.02),
        "conv_ln_b": nrm(ks[14], (N_CONV_LAYERS, CONV_CH), 0.02),
        "conv_w_out": nrm(ks[15], (N_CONV_LAYERS, CONV_CH, D), CONV_CH ** -0.5),
        "router_w": nrm(ks[16], (DEPTH, D, N_EXPERTS), D ** -0.5),
        "router_bias": nrm(ks[17], (DEPTH, N_EXPERTS), 0.01),
        "exp_w_gu": nrm(ks[18], (DEPTH, N_EXPERTS, D, 2 * EXPERT_FF), D ** -0.5),
        "exp_w_down": nrm(ks[19], (DEPTH, N_EXPERTS, EXPERT_FF, D), EXPERT_FF ** -0.5),
        "shared_w_gu": nrm(ks[20], (DEPTH, D, 2 * SHARED_FF), D ** -0.5),
        "shared_w_down": nrm(ks[21], (DEPTH, SHARED_FF, D), SHARED_FF ** -0.5),
        "final_norm": 1.0 + nrm(ks[22], (D,), 0.02),
    }


def reference(x, c, positions, mod_w, mod_b, norm_mix, norm_ffn, ret_w_in, ret_w_out,
              ret_out_gain, conv_w_in, conv_dw_w, conv_dw_b, conv_ln_g, conv_ln_b, conv_w_out,
              router_w, router_bias, exp_w_gu, exp_w_down, shared_w_gu, shared_w_down,
              final_norm):
    cond = jax.nn.silu(c)
    for i in range(DEPTH):
        mod = cond @ mod_w[i] + mod_b[i]
        sh1, sc1, g1, sh2, sc2, g2 = jnp.split(mod, 6, axis=-1)
        h = modulate(rms_norm(x, norm_mix[i]), sh1, sc1)
        j = i // N_MIXERS
        if i % N_MIXERS == 0:
            y = retention(h, positions, ret_w_in[j], ret_w_out[j], ret_out_gain[j])
        else:
            y = conv_module(h, conv_w_in[j], conv_dw_w[j], conv_dw_b[j], conv_ln_g[j],
                            conv_ln_b[j], conv_w_out[j])
        x = x + g1[:, None, :] * y
        h = modulate(rms_norm(x, norm_ffn[i]), sh2, sc2)
        x = x + g2[:, None, :] * moe(h, router_w[i], router_bias[i], exp_w_gu[i], exp_w_down[i],
                                     shared_w_gu[i], shared_w_down[i])
    return rms_norm(x, final_norm)
```

```python
import functools

import jax
import jax.numpy as jnp
from jax import lax
from jax.experimental import pallas as pl
from jax.experimental.pallas import tpu as pltpu

F32 = jnp.float32
BF16 = jnp.bfloat16
I32 = jnp.int32
U32 = jnp.uint32

RET_HEADS = 8
ROPE_BASE = 10000.0
CONV_WIDTH = 31
CONV_HALO = 32
N_GROUPS = 8
TOPK_GROUPS = 4
TOP_K = 8
ROUTED_SCALE = 2.5
NORM_EPS = 1e-6
MOD_ROWS = 8
LANES = 128
VMEM_LIMIT = 56 * 1024 * 1024


def _cparams(*sem):
    return pltpu.CompilerParams(dimension_semantics=sem, vmem_limit_bytes=VMEM_LIMIT)


def _tile(n, pref):
    t = min(n, pref)
    assert n % t == 0, (n, pref)
    return t


def _split_bf16(a):
    hi = a.astype(BF16)
    lo = (a - hi.astype(F32)).astype(BF16)
    return hi, lo


def _dot3(a, b, dims):
    ah, al = _split_bf16(a)
    bh, bl = _split_bf16(b)
    dg = functools.partial(lax.dot_general, dimension_numbers=dims, preferred_element_type=F32)
    return dg(ah, bh) + dg(ah, bl) + dg(al, bh)


_NN = (((1,), (0,)), ((), ()))
_NT = (((1,), (1,)), ((), ()))
_TN = (((0,), (0,)), ((), ()))


def _normmod(x, g, shift, scale):
    y = x * lax.rsqrt(jnp.mean(x * x, axis=-1, keepdims=True) + NORM_EPS)
    return (y * g) * (1.0 + scale) + shift


def _silu(x):
    return x * jax.nn.sigmoid(x)


def _pack_rows(h):
    half = h.shape[1] // 2
    bits = lax.bitcast_convert_type(h.astype(BF16).astype(F32), U32)
    lo = lax.shift_right_logical(bits[:, :half], jnp.uint32(16))
    hi = bits[:, half:] & jnp.uint32(0xFFFF0000)
    return hi | lo


def _unpack_rows(w):
    lo = lax.bitcast_convert_type(lax.shift_left(w, jnp.uint32(16)), F32)
    hi = lax.bitcast_convert_type(w & jnp.uint32(0xFFFF0000), F32)
    return lo, hi


def _unpack_rows_bf16(w):
    lo, hi = _unpack_rows(w)
    return jnp.concatenate([lo.astype(BF16), hi.astype(BF16)], axis=1)


def _mod_kernel(c_ref, w_ref, b_ref, o_ref):
    c = c_ref[...]
    o_ref[0] = _dot3(_silu(c), w_ref[0], _NN) + b_ref[0]


def _modulation(c, mod_w, mod_b):
    depth, d, n6 = mod_w.shape
    b = c.shape[0]
    assert b <= MOD_ROWS
    c_pad = jnp.zeros((MOD_ROWS, d), F32).at[:b].set(c)
    tn = _tile(n6, 1024)
    out = pl.pallas_call(
        _mod_kernel,
        grid=(depth, n6 // tn),
        in_specs=[pl.BlockSpec((MOD_ROWS, d), lambda i, j: (0, 0)),
                  pl.BlockSpec((1, d, tn), lambda i, j: (i, 0, j)),
                  pl.BlockSpec((1, 1, tn), lambda i, j: (i, 0, j))],
        out_specs=pl.BlockSpec((1, MOD_ROWS, tn), lambda i, j: (i, 0, j)),
        out_shape=jax.ShapeDtypeStruct((depth, MOD_ROWS, n6), F32),
        compiler_params=_cparams("arbitrary", "arbitrary"),
        name="adaln_modulation",
    )(c_pad, mod_w, mod_b.reshape(depth, 1, n6))
    return out[:, :b]


def _mod_block(mod_i, d):
    b = mod_i.shape[0]
    m = mod_i.reshape(b, 6, d)
    return jnp.concatenate([m, jnp.zeros((b, MOD_ROWS - 6, d), F32)], axis=1)


def _proj_kernel(x_ref, g_ref, mod_ref, w_ref, o_ref, h_scr):
    @pl.when(pl.program_id(1) == 0)
    def _():
        m = mod_ref[0]
        h_scr[...] = _normmod(x_ref[...], g_ref[...], m[0:1], m[1:2]).astype(BF16)

    o_ref[...] = jnp.dot(h_scr[...], w_ref[...], preferred_element_type=F32).astype(o_ref.dtype)


def _glu_proj_kernel(x_ref, g_ref, mod_ref, wa_ref, wb_ref, o_ref, h_scr):
    @pl.when(pl.program_id(1) == 0)
    def _():
        m = mod_ref[0]
        h_scr[...] = _normmod(x_ref[...], g_ref[...], m[0:1], m[1:2]).astype(BF16)

    h = h_scr[...]
    a = jnp.dot(h, wa_ref[...], preferred_element_type=F32)
    b = jnp.dot(h, wb_ref[...], preferred_element_type=F32)
    o_ref[...] = a * jax.nn.sigmoid(b)


def _normmod_proj(x, g, modb, w, seq, glu):
    n, d = x.shape
    nout = w.shape[1] // 2 if glu else w.shape[1]
    tm = _tile(seq, 1024)
    tn = _tile(nout, 512)
    tiles_per_seq = seq // tm
    x_spec = pl.BlockSpec((tm, d), lambda i, j: (i, 0))
    g_spec = pl.BlockSpec((1, d), lambda i, j: (0, 0))
    m_spec = pl.BlockSpec((1, MOD_ROWS, d), lambda i, j: (i // tiles_per_seq, 0, 0))
    if glu:
        half_blocks = nout // tn
        in_specs = [x_spec, g_spec, m_spec,
                    pl.BlockSpec((d, tn), lambda i, j: (0, j)),
                    pl.BlockSpec((d, tn), lambda i, j: (0, j + half_blocks))]
        body, args, odt = _glu_proj_kernel, (x, g, modb, w, w), F32
    else:
        in_specs = [x_spec, g_spec, m_spec, pl.BlockSpec((d, tn), lambda i, j: (0, j))]
        body, args, odt = _proj_kernel, (x, g, modb, w), BF16
    return pl.pallas_call(
        body,
        grid=(n // tm, nout // tn),
        in_specs=in_specs,
        out_specs=pl.BlockSpec((tm, tn), lambda i, j: (i, j)),
        out_shape=jax.ShapeDtypeStruct((n, nout), odt),
        scratch_shapes=[pltpu.VMEM((tm, d), BF16)],
        compiler_params=_cparams("arbitrary", "arbitrary"),
        name="glu_in_projection" if glu else "in_projection",
    )(*args)


def _retention_kernel(pos_ref, inv_ref, q_ref, k_ref, v_ref, g_ref, intra_ref, qd_ref, kd_ref,
                      cd_ref, gain_ref, o_ref, state, *, dk, dv):
    @pl.when(pl.program_id(1) == 0)
    def _():
        state[...] = jnp.zeros_like(state)

    half = dk // 2
    ang = pos_ref[...].astype(F32) * inv_ref[...]
    cos = jnp.cos(ang)
    sin = jnp.sin(ang)

    def rot(ref, h):
        x1 = ref[:, h * dk:h * dk + half].astype(F32)
        x2 = ref[:, h * dk + half:(h + 1) * dk].astype(F32)
        return jnp.concatenate([x1 * cos - x2 * sin, x1 * sin + x2 * cos], axis=1)

    for h in range(RET_HEADS):
        q = rot(q_ref, h)
        k = rot(k_ref, h) * (dk ** -0.5)
        v = v_ref[:, h * dv:(h + 1) * dv]
        s = lax.dot_general(q.astype(BF16), k.astype(BF16), _NT, preferred_element_type=F32)
        p = (s * intra_ref[h]).astype(BF16)
        inner = jnp.dot(p, v, preferred_element_type=F32)
        st = state[h]
        cross = jnp.dot((q * qd_ref[h]).astype(BF16), st.astype(BF16),
                        preferred_element_type=F32)
        kv = lax.dot_general((k * kd_ref[h]).astype(BF16), v, _TN, preferred_element_type=F32)
        state[h] = cd_ref[h][:, :1] * st + kv
        o = inner + cross
        o = o * lax.rsqrt(jnp.mean(o * o, axis=-1, keepdims=True) + NORM_EPS)
        gate = g_ref[:, h * dv:(h + 1) * dv].astype(F32)
        o_ref[:, h * dv:(h + 1) * dv] = (_silu(gate) * (o * gain_ref[:, h * dv:(h + 1) * dv])
                                         ).astype(o_ref.dtype)


def _retention_core(proj, positions, out_gain, batch, seq, d):
    n = batch * seq
    heads, dk = RET_HEADS, d // RET_HEADS
    dv = 2 * dk
    c = _tile(seq, 128)
    nc = seq // c
    half = dk // 2
    inv = (1.0 / (ROPE_BASE ** jnp.linspace(0.0, 1.0, half, dtype=F32))).reshape(1, half)
    log_gamma = jnp.log1p(-jnp.exp2(-5.0 - jnp.arange(heads, dtype=F32)))
    idx = jnp.arange(c, dtype=F32)
    rel = idx[:, None] - idx[None, :]
    intra = jnp.where(rel >= 0, jnp.exp(log_gamma[:, None, None] * jnp.maximum(rel, 0.0)), 0.0)
    q_decay = jnp.exp(log_gamma[:, None] * (idx + 1.0))[:, :, None]
    k_decay = jnp.exp(log_gamma[:, None] * (c - 1.0 - idx))[:, :, None]
    chunk_decay = jnp.broadcast_to(jnp.exp(log_gamma * c)[:, None, None], (heads, 1, LANES))
    row = lambda b, j: b * nc + j
    full3 = lambda b, j: (0, 0, 0)
    return pl.pallas_call(
        functools.partial(_retention_kernel, dk=dk, dv=dv),
        grid=(batch, nc),
        in_specs=[pl.BlockSpec((c, 1), lambda b, j: (row(b, j), 0)),
                  pl.BlockSpec((1, half), lambda b, j: (0, 0)),
                  pl.BlockSpec((c, heads * dk), lambda b, j: (row(b, j), 0)),
                  pl.BlockSpec((c, heads * dk), lambda b, j: (row(b, j), 1)),
                  pl.BlockSpec((c, heads * dv), lambda b, j: (row(b, j), 1)),
                  pl.BlockSpec((c, heads * dv), lambda b, j: (row(b, j), 2)),
                  pl.BlockSpec((heads, c, c), full3),
                  pl.BlockSpec((heads, c, 1), full3),
                  pl.BlockSpec((heads, c, 1), full3),
                  pl.BlockSpec((heads, 1, LANES), full3),
                  pl.BlockSpec((1, heads * dv), lambda b, j: (0, 0))],
        out_specs=pl.BlockSpec((c, heads * dv), lambda b, j: (row(b, j), 0)),
        out_shape=jax.ShapeDtypeStruct((n, heads * dv), BF16),
        scratch_shapes=[pltpu.VMEM((heads, dk, dv), F32)],
        compiler_params=_cparams("arbitrary", "arbitrary"),
        name="retention_core",
    )(positions.reshape(n, 1), inv, proj, proj, proj, proj, intra, q_decay, k_decay,
      chunk_decay, out_gain.reshape(1, heads * dv))


def _conv_kernel(u_ref, halo_ref, w_ref, b_ref, lg_ref, lb_ref, o_ref, ext, conv, *, tile):
    first = pl.program_id(1) == 0
    ext[0:CONV_HALO, :] = jnp.where(first, 0.0, halo_ref[...])
    ext[CONV_HALO:CONV_HALO + tile, :] = u_ref[...]
    base = CONV_HALO - (CONV_WIDTH - 1)
    ch = u_ref.shape[1]
    cw = min(ch, 4 * LANES)
    for c0 in range(0, ch, cw):
        part = jnp.zeros((tile, cw), F32) + b_ref[:, c0:c0 + cw]
        for j in range(CONV_WIDTH):
            part = part + ext[base + j:base + j + tile, c0:c0 + cw] * w_ref[j:j + 1, c0:c0 + cw]
        conv[:, c0:c0 + cw] = part
    acc = conv[...]
    mu = jnp.mean(acc, axis=-1, keepdims=True)
    cen = acc - mu
    var = jnp.mean(cen * cen, axis=-1, keepdims=True)
    y = cen * lax.rsqrt(var + NORM_EPS) * lg_ref[...] + lb_ref[...]
    o_ref[...] = _silu(y).astype(o_ref.dtype)


def _conv_ln_silu(u, dw_w, dw_b, ln_g, ln_b, batch, seq):
    n, ch = u.shape
    t = _tile(seq, 64)
    assert t % CONV_HALO == 0
    nt = seq // t
    halo_per_tile = t // CONV_HALO
    row = lambda b, j: b * nt + j
    vec = pl.BlockSpec((1, ch), lambda b, j: (0, 0))
    return pl.pallas_call(
        functools.partial(_conv_kernel, tile=t),
        grid=(batch, nt),
        in_specs=[pl.BlockSpec((t, ch), lambda b, j: (row(b, j), 0)),
                  pl.BlockSpec((CONV_HALO, ch),
                               lambda b, j: (jnp.maximum(row(b, j) * halo_per_tile - 1, 0), 0)),
                  pl.BlockSpec((CONV_WIDTH, ch), lambda b, j: (0, 0)),
                  vec, vec, vec],
        out_specs=pl.BlockSpec((t, ch), lambda b, j: (row(b, j), 0)),
        out_shape=jax.ShapeDtypeStruct((n, ch), BF16),
        scratch_shapes=[pltpu.VMEM((CONV_HALO + t, ch), F32), pltpu.VMEM((t, ch), F32)],
        compiler_params=_cparams("arbitrary", "arbitrary"),
        name="conv_ln_silu",
    )(u, u, dw_w, dw_b.reshape(1, ch), ln_g.reshape(1, ch), ln_b.reshape(1, ch))


def _outproj_kernel(y_ref, w_ref, x_ref, mod_ref, g_ref, rw_ref, x1_ref, h2_ref, lg_ref, acc):
    k = pl.program_id(1)

    @pl.when(k == 0)
    def _():
        acc[...] = jnp.zeros_like(acc)

    acc[...] += jnp.dot(y_ref[...], w_ref[...], preferred_element_type=F32)

    @pl.when(k == pl.num_programs(1) - 1)
    def _():
        m = mod_ref[0]
        x1 = x_ref[...] + m[2:3] * acc[...]
        x1_ref[...] = x1
        h2 = _normmod(x1, g_ref[...], m[3:4], m[4:5])
        h2_ref[...] = _pack_rows(h2)
        lg_ref[...] = _dot3(rw_ref[...], h2, _NT)


def _out_projection(y, w, x, modb, g_ffn, router_wt, seq):
    n, kdim = y.shape
    d = x.shape[1]
    ne = router_wt.shape[0]
    tm = _tile(seq, 512)
    tk = _tile(kdim, 512)
    tiles_per_seq = seq // tm
    return pl.pallas_call(
        _outproj_kernel,
        grid=(n // tm, kdim // tk),
        in_specs=[pl.BlockSpec((tm, tk), lambda i, k: (i, k)),
                  pl.BlockSpec((tk, d), lambda i, k: (k, 0)),
                  pl.BlockSpec((tm, d), lambda i, k: (i, 0)),
                  pl.BlockSpec((1, MOD_ROWS, d), lambda i, k: (i // tiles_per_seq, 0, 0)),
                  pl.BlockSpec((1, d), lambda i, k: (0, 0)),
                  pl.BlockSpec((ne, d), lambda i, k: (0, 0))],
        out_specs=[pl.BlockSpec((tm, d), lambda i, k: (i, 0)),
                   pl.BlockSpec((tm, d // 2), lambda i, k: (i, 0)),
                   pl.BlockSpec((ne, tm), lambda i, k: (0, i))],
        out_shape=[jax.ShapeDtypeStruct((n, d), F32),
                   jax.ShapeDtypeStruct((n, d // 2), U32),
                   jax.ShapeDtypeStruct((ne, n), F32)],
        scratch_shapes=[pltpu.VMEM((tm, d), F32)],
        compiler_params=_cparams("arbitrary", "arbitrary"),
        name="out_projection",
    )(y, w, x, modb, g_ffn, router_wt)


def _routing_kernel(lg_ref, bias_ref, eidx_ref, wts_ref):
    neg = -jnp.inf
    per = N_GROUPS
    tr = lg_ref.shape[1]
    scores = jax.nn.sigmoid(lg_ref[...])
    biased = scores + bias_ref[...]
    n_slab = lg_ref.shape[0] // per
    s_j = [scores[per * j:per * (j + 1), :] for j in range(n_slab)]
    b_j = [biased[per * j:per * (j + 1), :] for j in range(n_slab)]

    m1 = functools.reduce(jnp.maximum, b_j)
    ties = functools.reduce(lambda a, b: a + b, [(b == m1).astype(F32) for b in b_j])
    below = functools.reduce(jnp.maximum, [jnp.where(b < m1, b, neg) for b in b_j])
    grp = m1 + jnp.where(ties >= 2.0, m1, below)

    gid = lax.broadcasted_iota(I32, (per, tr), 0).astype(F32)
    chosen = jnp.zeros((per, tr), F32)
    cur = grp
    for _ in range(TOPK_GROUPS):
        mx = jnp.max(cur, axis=0, keepdims=True)
        first = jnp.min(jnp.where(cur == mx, gid, float(per)), axis=0, keepdims=True)
        pick = gid == first
        chosen = jnp.where(pick, 1.0, chosen)
        cur = jnp.where(pick, neg, cur)

    eid_j = [gid * float(n_slab) + float(j) for j in range(n_slab)]
    cur_j = [jnp.where(chosen > 0.0, b, neg) for b in b_j]
    picked_scores = []
    for k in range(TOP_K):
        mx = jnp.max(functools.reduce(jnp.maximum, cur_j), axis=0, keepdims=True)
        cand = functools.reduce(
            jnp.minimum, [jnp.where(c == mx, e, float(per * n_slab)) for c, e in zip(cur_j, eid_j)])
        first = jnp.min(cand, axis=0, keepdims=True)
        pick_j = [e == first for e in eid_j]
        sc = functools.reduce(lambda a, b: a + b,
                              [jnp.where(p, s, 0.0) for p, s in zip(pick_j, s_j)])
        picked_scores.append(jnp.sum(sc, axis=0, keepdims=True))
        cur_j = [jnp.where(p, neg, c) for p, c in zip(pick_j, cur_j)]
        eidx_ref[k:k + 1, :] = first.astype(I32)
    total = functools.reduce(lambda a, b: a + b, picked_scores)
    for k in range(TOP_K):
        wts_ref[k:k + 1, :] = picked_scores[k] / total * ROUTED_SCALE


def _routing(logits_t, bias_perm):
    ne, n = logits_t.shape
    tr = _tile(n, 1024)
    return pl.pallas_call(
        _routing_kernel,
        grid=(n // tr,),
        in_specs=[pl.BlockSpec((ne, tr), lambda i: (0, i)),
                  pl.BlockSpec((ne, 1), lambda i: (0, 0))],
        out_specs=[pl.BlockSpec((TOP_K, tr), lambda i: (0, i)),
                   pl.BlockSpec((TOP_K, tr), lambda i: (0, i))],
        out_shape=[jax.ShapeDtypeStruct((TOP_K, n), I32),
                   jax.ShapeDtypeStruct((TOP_K, n), F32)],
        compiler_params=_cparams("arbitrary"),
        name="moe_routing",
    )(logits_t, bias_perm.reshape(ne, 1))


def _plan_kernel(e_ref, u_ref, l_ref, dest_ref, be_ref, nu_ref, cnt, base, carry, *, tb, ne):
    p = pl.program_id(0)
    t = pl.program_id(1)
    tp = e_ref.shape[1]
    ei = e_ref[...]
    eid = lax.broadcasted_iota(I32, (ne, tp), 0)
    hit = [ei[k:k + 1, :] == eid for k in range(TOP_K)]
    onehot = functools.reduce(lambda a, b: a + b, [h.astype(F32) for h in hit])
    tile_cnt = jnp.sum(onehot, axis=1, keepdims=True)

    @pl.when((p == 0) & (t == 0))
    def _():
        cnt[...] = jnp.zeros_like(cnt)

    @pl.when(p == 0)
    def _():
        cnt[...] += tile_cnt

    @pl.when((p == 1) & (t == 0))
    def _():
        nblk = jnp.floor((cnt[...] + float(tb - 1)) * (1.0 / tb))
        start_blk = jnp.dot(l_ref[...], nblk.astype(BF16), preferred_element_type=F32)
        base[...] = start_blk * float(tb)
        carry[...] = jnp.zeros_like(carry)
        end_blk = start_blk + nblk
        nbp = be_ref.shape[1]
        blk = lax.broadcasted_iota(I32, (ne, nbp), 1).astype(F32)
        owner = jnp.sum((end_blk[:, :1] <= blk).astype(F32), axis=0, keepdims=True)
        be_ref[...] = jnp.minimum(owner, float(ne - 1)).astype(I32)
        nu_ref[...] = end_blk[ne - 1:ne, :].astype(I32)

    @pl.when(p == 1)
    def _():
        before = jnp.dot(onehot.astype(BF16), u_ref[...], preferred_element_type=F32)
        rowpos = base[:, :1] + carry[:, :1] + before
        for k in range(TOP_K):
            dest_ref[k:k + 1, :] = jnp.sum(jnp.where(hit[k], rowpos, 0.0), axis=0,
                                           keepdims=True).astype(I32)
        carry[...] += tile_cnt


def _dispatch_plan(eidx, ne, tb, n_blocks):
    n = eidx.shape[1]
    tp = _tile(n, 512)
    assert n // tb + 1 <= 256
    nbp = -(-n_blocks // LANES) * LANES
    upper = (jnp.arange(tp)[:, None] < jnp.arange(tp)[None, :]).astype(BF16)
    lower = (jnp.arange(ne)[None, :] < jnp.arange(ne)[:, None]).astype(BF16)
    return pl.pallas_call(
        functools.partial(_plan_kernel, tb=tb, ne=ne),
        grid=(2, n // tp),
        in_specs=[pl.BlockSpec((TOP_K, tp), lambda p, t: (0, t)),
                  pl.BlockSpec((tp, tp), lambda p, t: (0, 0)),
                  pl.BlockSpec((ne, ne), lambda p, t: (0, 0))],
        out_specs=[pl.BlockSpec((TOP_K, tp), lambda p, t: (0, t * p)),
                   pl.BlockSpec((1, nbp), lambda p, t: (0, 0)),
                   pl.BlockSpec((1, LANES), lambda p, t: (0, 0))],
        out_shape=[jax.ShapeDtypeStruct((TOP_K, n), I32),
                   jax.ShapeDtypeStruct((1, nbp), I32),
                   jax.ShapeDtypeStruct((1, LANES), I32)],
        scratch_shapes=[pltpu.VMEM((ne, LANES), F32)] * 3,
        compiler_params=_cparams("arbitrary", "arbitrary"),
        name="moe_dispatch_plan",
    )(eidx, upper, lower)


def _row_copy(src, s_row, dst, d_row, sem):
    return pltpu.make_async_copy(src.at[pl.ds(s_row, 1), :], dst.at[pl.ds(d_row, 1), :], sem)


def _dispatch_kernel(dest_ref, h_hbm, init_hbm, xs_hbm, sem, *, td):
    del init_hbm
    tok0 = pl.program_id(0) * td

    def issue(t, carry):
        for k in range(TOP_K):
            _row_copy(h_hbm, tok0 + t, xs_hbm, dest_ref[0, k, t], sem).start()
        return carry

    lax.fori_loop(0, td, issue, 0)

    def drain(t, carry):
        for _ in range(TOP_K):
            _row_copy(h_hbm, 0, xs_hbm, 0, sem).wait()
        return carry

    lax.fori_loop(0, td, drain, 0)


def _dispatch(dest3, h2p, n_rows):
    n, half = h2p.shape
    nt, _, td = dest3.shape
    return pl.pallas_call(
        functools.partial(_dispatch_kernel, td=td),
        grid=(nt,),
        in_specs=[pl.BlockSpec((1, TOP_K, td), lambda i: (i, 0, 0), memory_space=pltpu.SMEM),
                  pl.BlockSpec(memory_space=pl.ANY),
                  pl.BlockSpec(memory_space=pl.ANY)],
        out_specs=pl.BlockSpec(memory_space=pl.ANY),
        out_shape=jax.ShapeDtypeStruct((n_rows, half), U32),
        scratch_shapes=[pltpu.SemaphoreType.DMA(())],
        input_output_aliases={2: 0},
        compiler_params=_cparams("arbitrary"),
        name="moe_row_dispatch",
    )(dest3, h2p, jnp.zeros((n_rows, half), U32))


def _expert_kernel(be_ref, nu_ref, x_ref, wgu_ref, wdn_ref, o_ref):
    del be_ref

    @pl.when(pl.program_id(0) < nu_ref[0])
    def _():
        ff = wdn_ref.shape[1]
        x = _unpack_rows_bf16(x_ref[...])
        gu = jnp.dot(x, wgu_ref[0], preferred_element_type=F32)
        act = (_silu(gu[:, :ff]) * gu[:, ff:]).astype(BF16)
        o_ref[...] = _pack_rows(jnp.dot(act, wdn_ref[0], preferred_element_type=F32))


def _expert_ffn(xs, block_e, n_used, w_gu, w_dn, tb):
    n_rows, half = xs.shape
    ne, d, ff2 = w_gu.shape
    ff = w_dn.shape[1]
    nb = n_rows // tb
    live = lambda i, nu: jnp.minimum(i, nu[0] - 1)
    grid_spec = pltpu.PrefetchScalarGridSpec(
        num_scalar_prefetch=2,
        grid=(nb,),
        in_specs=[pl.BlockSpec((tb, half), lambda i, be, nu: (live(i, nu), 0)),
                  pl.BlockSpec((1, d, ff2), lambda i, be, nu: (be[live(i, nu)], 0, 0)),
                  pl.BlockSpec((1, ff, d), lambda i, be, nu: (be[live(i, nu)], 0, 0))],
        out_specs=pl.BlockSpec((tb, half), lambda i, be, nu: (live(i, nu), 0)))
    return pl.pallas_call(
        _expert_kernel,
        grid_spec=grid_spec,
        out_shape=jax.ShapeDtypeStruct((n_rows, half), U32),
        compiler_params=_cparams("arbitrary"),
        name="moe_expert_ffn",
    )(block_e, n_used, xs, w_gu, w_dn)


def _combine_kernel(dest_ref, wts_ref, ys_hbm, h_ref, x_ref, mod_ref, sgu_ref, sdn_ref, fn_ref,
                    o_ref, gbuf, sem, *, tc, final):
    def issue(t, carry):
        for k in range(TOP_K):
            _row_copy(ys_hbm, dest_ref[0, k, t], gbuf, k * tc + t, sem).start()
        return carry

    lax.fori_loop(0, tc, issue, 0)

    ff = sdn_ref.shape[0]
    gu = jnp.dot(_unpack_rows_bf16(h_ref[...]), sgu_ref[...], preferred_element_type=F32)
    act = (_silu(gu[:, :ff]) * gu[:, ff:]).astype(BF16)
    shared = jnp.dot(act, sdn_ref[...], preferred_element_type=F32)

    def drain(t, carry):
        for _ in range(TOP_K):
            _row_copy(ys_hbm, 0, gbuf, 0, sem).wait()
        return carry

    lax.fori_loop(0, tc, drain, 0)

    w = wts_ref[...]
    lo, hi = _unpack_rows(gbuf[0:tc, :])
    acc_lo, acc_hi = w[:, 0:1] * lo, w[:, 0:1] * hi
    for k in range(1, TOP_K):
        lo, hi = _unpack_rows(gbuf[k * tc:(k + 1) * tc, :])
        acc_lo = acc_lo + w[:, k:k + 1] * lo
        acc_hi = acc_hi + w[:, k:k + 1] * hi
    moe = jnp.concatenate([acc_lo, acc_hi], axis=1) + shared
    x2 = x_ref[...] + mod_ref[0][5:6] * moe
    if final:
        x2 = x2 * lax.rsqrt(jnp.mean(x2 * x2, axis=-1, keepdims=True) + NORM_EPS) * fn_ref[...]
    o_ref[...] = x2


def _combine(dest3, wts_t, ys, h2p, x1, modb, s_gu, s_dn, final_norm, seq, final):
    n, d = x1.shape
    nt, _, tc = dest3.shape
    half = d // 2
    tiles_per_seq = seq // tc
    full2 = lambda i: (0, 0)
    return pl.pallas_call(
        functools.partial(_combine_kernel, tc=tc, final=final),
        grid=(nt,),
        in_specs=[pl.BlockSpec((1, TOP_K, tc), lambda i: (i, 0, 0), memory_space=pltpu.SMEM),
                  pl.BlockSpec((tc, TOP_K), lambda i: (i, 0)),
                  pl.BlockSpec(memory_space=pl.ANY),
                  pl.BlockSpec((tc, half), lambda i: (i, 0)),
                  pl.BlockSpec((tc, d), lambda i: (i, 0)),
                  pl.BlockSpec((1, MOD_ROWS, d), lambda i: (i // tiles_per_seq, 0, 0)),
                  pl.BlockSpec(s_gu.shape, full2),
                  pl.BlockSpec(s_dn.shape, full2),
                  pl.BlockSpec((1, d), full2)],
        out_specs=pl.BlockSpec((tc, d), lambda i: (i, 0)),
        out_shape=jax.ShapeDtypeStruct((n, d), F32),
        scratch_shapes=[pltpu.VMEM((TOP_K * tc, half), U32), pltpu.SemaphoreType.DMA(())],
        compiler_params=_cparams("arbitrary"),
        name="moe_combine",
    )(dest3, wts_t, ys, h2p, x1, modb, s_gu, s_dn, final_norm.reshape(1, d))


def _moe(x1, h2p, logits_t, modb, bias_perm, w_gu, w_dn, s_gu, s_dn, final_norm, seq, final):
    n, d = x1.shape
    ne = w_gu.shape[0]
    tb = 256
    tok_tile = _tile(seq, 256)
    n_blocks = n * TOP_K // tb + ne
    eidx, wts = _routing(logits_t, bias_perm)
    dest, block_e, n_used = _dispatch_plan(eidx, ne, tb, n_blocks)
    dest3 = dest.reshape(TOP_K, n // tok_tile, tok_tile).transpose(1, 0, 2)
    xs = _dispatch(dest3, h2p, n_blocks * tb)
    ys = _expert_ffn(xs, block_e[0, :n_blocks], n_used[0, :1], w_gu, w_dn, tb)
    return _combine(dest3, wts.T, ys, h2p, x1, modb, s_gu, s_dn, final_norm, seq, final)


def _expert_major_rows(a):
    per = a.shape[0] // N_GROUPS
    return a.reshape((N_GROUPS, per) + a.shape[1:]).swapaxes(0, 1).reshape(a.shape)


def _pairs_to_halves(w, heads):
    d, cols = w.shape
    dk = cols // heads
    return w.reshape(d, heads, dk // 2, 2).transpose(0, 1, 3, 2).reshape(d, cols)


def kernel(x, c, positions, mod_w, mod_b, norm_mix, norm_ffn, ret_w_in, ret_w_out, ret_out_gain, conv_w_in, conv_dw_w, conv_dw_b, conv_ln_g, conv_ln_b, conv_w_out, router_w, router_bias, exp_w_gu, exp_w_down, shared_w_gu, shared_w_down, final_norm):
    batch, seq, d = x.shape
    n = batch * seq
    depth = mod_w.shape[0]
    heads = RET_HEADS
    qk_cols = d
    mods = _modulation(c, mod_w, mod_b)
    xt = x.reshape(n, d)
    for i in range(depth):
        modb = _mod_block(mods[i], d)
        g_mix = norm_mix[i].reshape(1, d)
        g_ffn = norm_ffn[i].reshape(1, d)
        j = i // 2
        if i % 2 == 0:
            w_in = ret_w_in[j]
            w_in = jnp.concatenate([_pairs_to_halves(w_in[:, :qk_cols], heads),
                                    _pairs_to_halves(w_in[:, qk_cols:2 * qk_cols], heads),
                                    w_in[:, 2 * qk_cols:]], axis=1).astype(BF16)
            proj = _normmod_proj(xt, g_mix, modb, w_in, seq, glu=False)
            y = _retention_core(proj, positions, ret_out_gain[j], batch, seq, d)
            w_out = ret_w_out[j].astype(BF16)
        else:
            u = _normmod_proj(xt, g_mix, modb, conv_w_in[j].astype(BF16), seq, glu=True)
            y = _conv_ln_silu(u, conv_dw_w[j], conv_dw_b[j], conv_ln_g[j], conv_ln_b[j], batch, seq)
            w_out = conv_w_out[j].astype(BF16)
        router_wt = _expert_major_rows(router_w[i].T)
        x1, h2p, logits_t = _out_projection(y, w_out, xt, modb, g_ffn, router_wt, seq)
        xt = _moe(x1, h2p, logits_t, modb, _expert_major_rows(router_bias[i]),
                  exp_w_gu[i].astype(BF16), exp_w_down[i].astype(BF16),
                  shared_w_gu[i].astype(BF16), shared_w_down[i].astype(BF16),
                  final_norm, seq, final=(i == depth - 1))
    return xt.reshape(batch, seq, d)
```

```python
import functools

import jax
import jax.numpy as jnp
from jax import lax
from jax.experimental import pallas as pl
from jax.experimental.pallas import tpu as pltpu

F32 = jnp.float32
BF16 = jnp.bfloat16
I32 = jnp.int32
U32 = jnp.uint32

RET_HEADS = 8
ROPE_BASE = 10000.0
CONV_WIDTH = 31
CONV_HALO = 32
N_GROUPS = 8
TOPK_GROUPS = 4
TOP_K = 8
ROUTED_SCALE = 2.5
NORM_EPS = 1e-6
MOD_ROWS = 8
LANES = 128
VMEM_LIMIT = 56 * 1024 * 1024


def _cparams(*sem):
    return pltpu.CompilerParams(dimension_semantics=sem, vmem_limit_bytes=VMEM_LIMIT)


def _tile(n, pref):
    t = min(n, pref)
    assert n % t == 0, (n, pref)
    return t


def _split_bf16(a):
    hi = a.astype(BF16)
    lo = (a - hi.astype(F32)).astype(BF16)
    return hi, lo


def _dot3(a, b, dims):
    ah, al = _split_bf16(a)
    bh, bl = _split_bf16(b)
    dg = functools.partial(lax.dot_general, dimension_numbers=dims, preferred_element_type=F32)
    return dg(ah, bh) + dg(ah, bl) + dg(al, bh)


_NN = (((1,), (0,)), ((), ()))
_NT = (((1,), (1,)), ((), ()))
_TN = (((0,), (0,)), ((), ()))


def _normmod(x, g, shift, scale):
    y = x * lax.rsqrt(jnp.mean(x * x, axis=-1, keepdims=True) + NORM_EPS)
    return (y * g) * (1.0 + scale) + shift


def _silu(x):
    return x * jax.nn.sigmoid(x)


def _pack_rows(h):
    half = h.shape[1] // 2
    bits = lax.bitcast_convert_type(h.astype(BF16).astype(F32), U32)
    lo = lax.shift_right_logical(bits[:, :half], jnp.uint32(16))
    hi = bits[:, half:] & jnp.uint32(0xFFFF0000)
    return hi | lo


def _unpack_rows(w):
    lo = lax.bitcast_convert_type(lax.shift_left(w, jnp.uint32(16)), F32)
    hi = lax.bitcast_convert_type(w & jnp.uint32(0xFFFF0000), F32)
    return lo, hi


def _unpack_rows_bf16(w):
    lo, hi = _unpack_rows(w)
    return jnp.concatenate([lo.astype(BF16), hi.astype(BF16)], axis=1)


def _mod_kernel(c_ref, w_ref, b_ref, o_ref):
    c = c_ref[...]
    o_ref[0] = _dot3(_silu(c), w_ref[0], _NN) + b_ref[0]


def _modulation(c, mod_w, mod_b):
    depth, d, n6 = mod_w.shape
    b = c.shape[0]
    assert b <= MOD_ROWS
    c_pad = jnp.zeros((MOD_ROWS, d), F32).at[:b].set(c)
    tn = _tile(n6, 1024)
    out = pl.pallas_call(
        _mod_kernel,
        grid=(depth, n6 // tn),
        in_specs=[pl.BlockSpec((MOD_ROWS, d), lambda i, j: (0, 0)),
                  pl.BlockSpec((1, d, tn), lambda i, j: (i, 0, j)),
                  pl.BlockSpec((1, 1, tn), lambda i, j: (i, 0, j))],
        out_specs=pl.BlockSpec((1, MOD_ROWS, tn), lambda i, j: (i, 0, j)),
        out_shape=jax.ShapeDtypeStruct((depth, MOD_ROWS, n6), F32),
        compiler_params=_cparams("arbitrary", "arbitrary"),
        name="adaln_modulation",
    )(c_pad, mod_w, mod_b.reshape(depth, 1, n6))
    return out[:, :b]


def _mod_block(mod_i, d):
    b = mod_i.shape[0]
    m = mod_i.reshape(b, 6, d)
    return jnp.concatenate([m, jnp.zeros((b, MOD_ROWS - 6, d), F32)], axis=1)


def _proj_kernel(x_ref, g_ref, mod_ref, w_ref, o_ref, h_scr):
    @pl.when(pl.program_id(1) == 0)
    def _():
        m = mod_ref[0]
        h_scr[...] = _normmod(x_ref[...], g_ref[...], m[0:1], m[1:2]).astype(BF16)

    o_ref[...] = jnp.dot(h_scr[...], w_ref[...], preferred_element_type=F32).astype(o_ref.dtype)


def _glu_proj_kernel(x_ref, g_ref, mod_ref, wa_ref, wb_ref, o_ref, h_scr):
    @pl.when(pl.program_id(1) == 0)
    def _():
        m = mod_ref[0]
        h_scr[...] = _normmod(x_ref[...], g_ref[...], m[0:1], m[1:2]).astype(BF16)

    h = h_scr[...]
    a = jnp.dot(h, wa_ref[...], preferred_element_type=F32)
    b = jnp.dot(h, wb_ref[...], preferred_element_type=F32)
    o_ref[...] = a * jax.nn.sigmoid(b)


def _normmod_proj(x, g, modb, w, seq, glu):
    n, d = x.shape
    nout = w.shape[1] // 2 if glu else w.shape[1]
    tm = _tile(seq, 1024)
    tn = _tile(nout, 512)
    tiles_per_seq = seq // tm
    x_spec = pl.BlockSpec((tm, d), lambda i, j: (i, 0))
    g_spec = pl.BlockSpec((1, d), lambda i, j: (0, 0))
    m_spec = pl.BlockSpec((1, MOD_ROWS, d), lambda i, j: (i // tiles_per_seq, 0, 0))
    if glu:
        half_blocks = nout // tn
        in_specs = [x_spec, g_spec, m_spec,
                    pl.BlockSpec((d, tn), lambda i, j: (0, j)),
                    pl.BlockSpec((d, tn), lambda i, j: (0, j + half_blocks))]
        body, args, odt = _glu_proj_kernel, (x, g, modb, w, w), F32
    else:
        in_specs = [x_spec, g_spec, m_spec, pl.BlockSpec((d, tn), lambda i, j: (0, j))]
        body, args, odt = _proj_kernel, (x, g, modb, w), BF16
    return pl.pallas_call(
        body,
        grid=(n // tm, nout // tn),
        in_specs=in_specs,
        out_specs=pl.BlockSpec((tm, tn), lambda i, j: (i, j)),
        out_shape=jax.ShapeDtypeStruct((n, nout), odt),
        scratch_shapes=[pltpu.VMEM((tm, d), BF16)],
        compiler_params=_cparams("arbitrary", "arbitrary"),
        name="glu_in_projection" if glu else "in_projection",
    )(*args)


def _retention_kernel(pos_ref, inv_ref, q_ref, k_ref, v_ref, g_ref, intra_ref, qd_ref, kd_ref,
                      cd_ref, gain_ref, o_ref, state, *, dk, dv):
    @pl.when(pl.program_id(1) == 0)
    def _():
        state[...] = jnp.zeros_like(state)

    half = dk // 2
    ang = pos_ref[...].astype(F32) * inv_ref[...]
    cos = jnp.cos(ang)
    sin = jnp.sin(ang)

    def rot(ref, h):
        x1 = ref[:, h * dk:h * dk + half].astype(F32)
        x2 = ref[:, h * dk + half:(h + 1) * dk].astype(F32)
        return jnp.concatenate([x1 * cos - x2 * sin, x1 * sin + x2 * cos], axis=1)

    for h in range(RET_HEADS):
        q = rot(q_ref, h)
        k = rot(k_ref, h) * (dk ** -0.5)
        v = v_ref[:, h * dv:(h + 1) * dv]
        s = lax.dot_general(q.astype(BF16), k.astype(BF16), _NT, preferred_element_type=F32)
        p = (s * intra_ref[h]).astype(BF16)
        inner = jnp.dot(p, v, preferred_element_type=F32)
        st = state[h]
        cross = jnp.dot((q * qd_ref[h]).astype(BF16), st.astype(BF16),
                        preferred_element_type=F32)
        kv = lax.dot_general((k * kd_ref[h]).astype(BF16), v, _TN, preferred_element_type=F32)
        state[h] = cd_ref[h][:, :1] * st + kv
        o = inner + cross
        o = o * lax.rsqrt(jnp.mean(o * o, axis=-1, keepdims=True) + NORM_EPS)
        gate = g_ref[:, h * dv:(h + 1) * dv].astype(F32)
        o_ref[:, h * dv:(h + 1) * dv] = (_silu(gate) * (o * gain_ref[:, h * dv:(h + 1) * dv])
                                         ).astype(o_ref.dtype)


def _retention_core(proj, positions, out_gain, batch, seq, d):
    n = batch * seq
    heads, dk = RET_HEADS, d // RET_HEADS
    dv = 2 * dk
    c = _tile(seq, 128)
    nc = seq // c
    half = dk // 2
    inv = (1.0 / (ROPE_BASE ** jnp.linspace(0.0, 1.0, half, dtype=F32))).reshape(1, half)
    log_gamma = jnp.log1p(-jnp.exp2(-5.0 - jnp.arange(heads, dtype=F32)))
    idx = jnp.arange(c, dtype=F32)
    rel = idx[:, None] - idx[None, :]
    intra = jnp.where(rel >= 0, jnp.exp(log_gamma[:, None, None] * jnp.maximum(rel, 0.0)), 0.0)
    q_decay = jnp.exp(log_gamma[:, None] * (idx + 1.0))[:, :, None]
    k_decay = jnp.exp(log_gamma[:, None] * (c - 1.0 - idx))[:, :, None]
    chunk_decay = jnp.broadcast_to(jnp.exp(log_gamma * c)[:, None, None], (heads, 1, LANES))
    row = lambda b, j: b * nc + j
    full3 = lambda b, j: (0, 0, 0)
    return pl.pallas_call(
        functools.partial(_retention_kernel, dk=dk, dv=dv),
        grid=(batch, nc),
        in_specs=[pl.BlockSpec((c, 1), lambda b, j: (row(b, j), 0)),
                  pl.BlockSpec((1, half), lambda b, j: (0, 0)),
                  pl.BlockSpec((c, heads * dk), lambda b, j: (row(b, j), 0)),
                  pl.BlockSpec((c, heads * dk), lambda b, j: (row(b, j), 1)),
                  pl.BlockSpec((c, heads * dv), lambda b, j: (row(b, j), 1)),
                  pl.BlockSpec((c, heads * dv), lambda b, j: (row(b, j), 2)),
                  pl.BlockSpec((heads, c, c), full3),
                  pl.BlockSpec((heads, c, 1), full3),
                  pl.BlockSpec((heads, c, 1), full3),
                  pl.BlockSpec((heads, 1, LANES), full3),
                  pl.BlockSpec((1, heads * dv), lambda b, j: (0, 0))],
        out_specs=pl.BlockSpec((c, heads * dv), lambda b, j: (row(b, j), 0)),
        out_shape=jax.ShapeDtypeStruct((n, heads * dv), BF16),
        scratch_shapes=[pltpu.VMEM((heads, dk, dv), F32)],
        compiler_params=_cparams("arbitrary", "arbitrary"),
        name="retention_core",
    )(positions.reshape(n, 1), inv, proj, proj, proj, proj, intra, q_decay, k_decay,
      chunk_decay, out_gain.reshape(1, heads * dv))


def _conv_kernel(u_ref, halo_ref, w_ref, b_ref, lg_ref, lb_ref, o_ref, ext, conv, *, tile):
    first = pl.program_id(1) == 0
    ext[0:CONV_HALO, :] = jnp.where(first, 0.0, halo_ref[...])
    ext[CONV_HALO:CONV_HALO + tile, :] = u_ref[...]
    base = CONV_HALO - (CONV_WIDTH - 1)
    ch = u_ref.shape[1]
    cw = min(ch, 4 * LANES)
    for c0 in range(0, ch, cw):
        part = jnp.zeros((tile, cw), F32) + b_ref[:, c0:c0 + cw]
        for j in range(CONV_WIDTH):
            part = part + ext[base + j:base + j + tile, c0:c0 + cw] * w_ref[j:j + 1, c0:c0 + cw]
        conv[:, c0:c0 + cw] = part
    acc = conv[...]
    mu = jnp.mean(acc, axis=-1, keepdims=True)
    cen = acc - mu
    var = jnp.mean(cen * cen, axis=-1, keepdims=True)
    y = cen * lax.rsqrt(var + NORM_EPS) * lg_ref[...] + lb_ref[...]
    o_ref[...] = _silu(y).astype(o_ref.dtype)


def _conv_ln_silu(u, dw_w, dw_b, ln_g, ln_b, batch, seq):
    n, ch = u.shape
    t = _tile(seq, 64)
    assert t % CONV_HALO == 0
    nt = seq // t
    halo_per_tile = t // CONV_HALO
    row = lambda b, j: b * nt + j
    vec = pl.BlockSpec((1, ch), lambda b, j: (0, 0))
    return pl.pallas_call(
        functools.partial(_conv_kernel, tile=t),
        grid=(batch, nt),
        in_specs=[pl.BlockSpec((t, ch), lambda b, j: (row(b, j), 0)),
                  pl.BlockSpec((CONV_HALO, ch),
                               lambda b, j: (jnp.maximum(row(b, j) * halo_per_tile - 1, 0), 0)),
                  pl.BlockSpec((CONV_WIDTH, ch), lambda b, j: (0, 0)),
                  vec, vec, vec],
        out_specs=pl.BlockSpec((t, ch), lambda b, j: (row(b, j), 0)),
        out_shape=jax.ShapeDtypeStruct((n, ch), BF16),
        scratch_shapes=[pltpu.VMEM((CONV_HALO + t, ch), F32), pltpu.VMEM((t, ch), F32)],
        compiler_params=_cparams("arbitrary", "arbitrary"),
        name="conv_ln_silu",
    )(u, u, dw_w, dw_b.reshape(1, ch), ln_g.reshape(1, ch), ln_b.reshape(1, ch))


def _outproj_kernel(y_ref, w_ref, x_ref, mod_ref, g_ref, rw_ref, x1_ref, h2_ref, lg_ref, acc):
    k = pl.program_id(1)

    @pl.when(k == 0)
    def _():
        acc[...] = jnp.zeros_like(acc)

    acc[...] += jnp.dot(y_ref[...], w_ref[...], preferred_element_type=F32)

    @pl.when(k == pl.num_programs(1) - 1)
    def _():
        m = mod_ref[0]
        x1 = x_ref[...] + m[2:3] * acc[...]
        x1_ref[...] = x1
        h2 = _normmod(x1, g_ref[...], m[3:4], m[4:5])
        h2_ref[...] = _pack_rows(h2)
        lg_ref[...] = _dot3(rw_ref[...], h2, _NT)


def _out_projection(y, w, x, modb, g_ffn, router_wt, seq):
    n, kdim = y.shape
    d = x.shape[1]
    ne = router_wt.shape[0]
    tm = _tile(seq, 512)
    tk = _tile(kdim, 512)
    tiles_per_seq = seq // tm
    return pl.pallas_call(
        _outproj_kernel,
        grid=(n // tm, kdim // tk),
        in_specs=[pl.BlockSpec((tm, tk), lambda i, k: (i, k)),
                  pl.BlockSpec((tk, d), lambda i, k: (k, 0)),
                  pl.BlockSpec((tm, d), lambda i, k: (i, 0)),
                  pl.BlockSpec((1, MOD_ROWS, d), lambda i, k: (i // tiles_per_seq, 0, 0)),
                  pl.BlockSpec((1, d), lambda i, k: (0, 0)),
                  pl.BlockSpec((ne, d), lambda i, k: (0, 0))],
        out_specs=[pl.BlockSpec((tm, d), lambda i, k: (i, 0)),
                   pl.BlockSpec((tm, d // 2), lambda i, k: (i, 0)),
                   pl.BlockSpec((ne, tm), lambda i, k: (0, i))],
        out_shape=[jax.ShapeDtypeStruct((n, d), F32),
                   jax.ShapeDtypeStruct((n, d // 2), U32),
                   jax.ShapeDtypeStruct((ne, n), F32)],
        scratch_shapes=[pltpu.VMEM((tm, d), F32)],
        compiler_params=_cparams("arbitrary", "arbitrary"),
        name="out_projection",
    )(y, w, x, modb, g_ffn, router_wt)


def _routing_kernel(lg_ref, bias_ref, eidx_ref, wts_ref):
    neg = -jnp.inf
    per = N_GROUPS
    tr = lg_ref.shape[1]
    scores = jax.nn.sigmoid(lg_ref[...])
    biased = scores + bias_ref[...]
    n_slab = lg_ref.shape[0] // per
    s_j = [scores[per * j:per * (j + 1), :] for j in range(n_slab)]
    b_j = [biased[per * j:per * (j + 1), :] for j in range(n_slab)]

    m1 = functools.reduce(jnp.maximum, b_j)
    ties = functools.reduce(lambda a, b: a + b, [(b == m1).astype(F32) for b in b_j])
    below = functools.reduce(jnp.maximum, [jnp.where(b < m1, b, neg) for b in b_j])
    grp = m1 + jnp.where(ties >= 2.0, m1, below)

    gid = lax.broadcasted_iota(I32, (per, tr), 0).astype(F32)
    chosen = jnp.zeros((per, tr), F32)
    cur = grp
    for _ in range(TOPK_GROUPS):
        mx = jnp.max(cur, axis=0, keepdims=True)
        first = jnp.min(jnp.where(cur == mx, gid, float(per)), axis=0, keepdims=True)
        pick = gid == first
        chosen = jnp.where(pick, 1.0, chosen)
        cur = jnp.where(pick, neg, cur)

    eid_j = [gid * float(n_slab) + float(j) for j in range(n_slab)]
    cur_j = [jnp.where(chosen > 0.0, b, neg) for b in b_j]
    picked_scores = []
    for k in range(TOP_K):
        mx = jnp.max(functools.reduce(jnp.maximum, cur_j), axis=0, keepdims=True)
        cand = functools.reduce(
            jnp.minimum, [jnp.where(c == mx, e, float(per * n_slab)) for c, e in zip(cur_j, eid_j)])
        first = jnp.min(cand, axis=0, keepdims=True)
        pick_j = [e == first for e in eid_j]
        sc = functools.reduce(lambda a, b: a + b,
                              [jnp.where(p, s, 0.0) for p, s in zip(pick_j, s_j)])
        picked_scores.append(jnp.sum(sc, axis=0, keepdims=True))
        cur_j = [jnp.where(p, neg, c) for p, c in zip(pick_j, cur_j)]
        eidx_ref[k:k + 1, :] = first.astype(I32)
    total = functools.reduce(lambda a, b: a + b, picked_scores)
    for k in range(TOP_K):
        wts_ref[k:k + 1, :] = picked_scores[k] / total * ROUTED_SCALE


def _routing(logits_t, bias_perm):
    ne, n = logits_t.shape
    tr = _tile(n, 1024)
    return pl.pallas_call(
        _routing_kernel,
        grid=(n // tr,),
        in_specs=[pl.BlockSpec((ne, tr), lambda i: (0, i)),
                  pl.BlockSpec((ne, 1), lambda i: (0, 0))],
        out_specs=[pl.BlockSpec((TOP_K, tr), lambda i: (0, i)),
                   pl.BlockSpec((TOP_K, tr), lambda i: (0, i))],
        out_shape=[jax.ShapeDtypeStruct((TOP_K, n), I32),
                   jax.ShapeDtypeStruct((TOP_K, n), F32)],
        compiler_params=_cparams("arbitrary"),
        name="moe_routing",
    )(logits_t, bias_perm.reshape(ne, 1))


def _plan_kernel(e_ref, u_ref, l_ref, dest_ref, be_ref, nu_ref, cnt, base, carry, *, tb, ne):
    p = pl.program_id(0)
    t = pl.program_id(1)
    tp = e_ref.shape[1]
    ei = e_ref[...]
    eid = lax.broadcasted_iota(I32, (ne, tp), 0)
    hit = [ei[k:k + 1, :] == eid for k in range(TOP_K)]
    onehot = functools.reduce(lambda a, b: a + b, [h.astype(F32) for h in hit])
    tile_cnt = jnp.sum(onehot, axis=1, keepdims=True)

    @pl.when((p == 0) & (t == 0))
    def _():
        cnt[...] = jnp.zeros_like(cnt)

    @pl.when(p == 0)
    def _():
        cnt[...] += tile_cnt

    @pl.when((p == 1) & (t == 0))
    def _():
        nblk = jnp.floor((cnt[...] + float(tb - 1)) * (1.0 / tb))
        start_blk = jnp.dot(l_ref[...], nblk.astype(BF16), preferred_element_type=F32)
        base[...] = start_blk * float(tb)
        carry[...] = jnp.zeros_like(carry)
        end_blk = start_blk + nblk
        nbp = be_ref.shape[1]
        blk = lax.broadcasted_iota(I32, (ne, nbp), 1).astype(F32)
        owner = jnp.sum((end_blk[:, :1] <= blk).astype(F32), axis=0, keepdims=True)
        be_ref[...] = jnp.minimum(owner, float(ne - 1)).astype(I32)
        nu_ref[...] = end_blk[ne - 1:ne, :].astype(I32)

    @pl.when(p == 1)
    def _():
        before = jnp.dot(onehot.astype(BF16), u_ref[...], preferred_element_type=F32)
        rowpos = base[:, :1] + carry[:, :1] + before
        for k in range(TOP_K):
            dest_ref[k:k + 1, :] = jnp.sum(jnp.where(hit[k], rowpos, 0.0), axis=0,
                                           keepdims=True).astype(I32)
        carry[...] += tile_cnt


def _dispatch_plan(eidx, ne, tb, n_blocks):
    n = eidx.shape[1]
    tp = _tile(n, 512)
    assert n // tb + 1 <= 256
    nbp = -(-n_blocks // LANES) * LANES
    upper = (jnp.arange(tp)[:, None] < jnp.arange(tp)[None, :]).astype(BF16)
    lower = (jnp.arange(ne)[None, :] < jnp.arange(ne)[:, None]).astype(BF16)
    return pl.pallas_call(
        functools.partial(_plan_kernel, tb=tb, ne=ne),
        grid=(2, n // tp),
        in_specs=[pl.BlockSpec((TOP_K, tp), lambda p, t: (0, t)),
                  pl.BlockSpec((tp, tp), lambda p, t: (0, 0)),
                  pl.BlockSpec((ne, ne), lambda p, t: (0, 0))],
        out_specs=[pl.BlockSpec((TOP_K, tp), lambda p, t: (0, t * p)),
                   pl.BlockSpec((1, nbp), lambda p, t: (0, 0)),
                   pl.BlockSpec((1, LANES), lambda p, t: (0, 0))],
        out_shape=[jax.ShapeDtypeStruct((TOP_K, n), I32),
                   jax.ShapeDtypeStruct((1, nbp), I32),
                   jax.ShapeDtypeStruct((1, LANES), I32)],
        scratch_shapes=[pltpu.VMEM((ne, LANES), F32)] * 3,
        compiler_params=_cparams("arbitrary", "arbitrary"),
        name="moe_dispatch_plan",
    )(eidx, upper, lower)


def _row_copy(src, s_row, dst, d_row, sem):
    return pltpu.make_async_copy(src.at[pl.ds(s_row, 1), :], dst.at[pl.ds(d_row, 1), :], sem)


def _dispatch_kernel(dest_ref, h_ref, init_hbm, xs_hbm, sem, *, td):
    del init_hbm

    def issue(t, carry):
        for k in range(TOP_K):
            _row_copy(h_ref, t, xs_hbm, dest_ref[0, k, t], sem).start()
        return carry

    lax.fori_loop(0, td, issue, 0)

    def drain(t, carry):
        for _ in range(TOP_K):
            _row_copy(h_ref, 0, xs_hbm, 0, sem).wait()
        return carry

    lax.fori_loop(0, td, drain, 0)


def _dispatch(dest3, h2p, n_rows):
    n, half = h2p.shape
    nt, _, td = dest3.shape
    return pl.pallas_call(
        functools.partial(_dispatch_kernel, td=td),
        grid=(nt,),
        in_specs=[pl.BlockSpec((1, TOP_K, td), lambda i: (i, 0, 0), memory_space=pltpu.SMEM),
                  pl.BlockSpec((td, half), lambda i: (i, 0)),
                  pl.BlockSpec(memory_space=pl.ANY)],
        out_specs=pl.BlockSpec(memory_space=pl.ANY),
        out_shape=jax.ShapeDtypeStruct((n_rows, half), U32),
        scratch_shapes=[pltpu.SemaphoreType.DMA(())],
        input_output_aliases={2: 0},
        compiler_params=_cparams("arbitrary"),
        name="moe_row_dispatch",
    )(dest3, h2p, jnp.zeros((n_rows, half), U32))


def _expert_kernel(be_ref, nu_ref, x_ref, wgu_ref, wdn_ref, o_ref, wgu_b, wdn_b):
    i = pl.program_id(0)
    live = i < nu_ref[0]
    new_expert = (i == 0) | (be_ref[i] != be_ref[jnp.maximum(i - 1, 0)])

    @pl.when(live & new_expert)
    def _():
        wgu_b[...] = wgu_ref[0, 0].astype(BF16)
        wdn_b[...] = wdn_ref[0, 0].astype(BF16)

    @pl.when(live)
    def _():
        ff = wdn_b.shape[0]
        x = _unpack_rows_bf16(x_ref[...])
        gu = jnp.dot(x, wgu_b[...], preferred_element_type=F32)
        act = (_silu(gu[:, :ff]) * gu[:, ff:]).astype(BF16)
        o_ref[...] = _pack_rows(jnp.dot(act, wdn_b[...], preferred_element_type=F32))


def _expert_ffn(xs, block_e, n_used, w_gu, w_dn, layer, tb):
    n_rows, half = xs.shape
    _, _, d, ff2 = w_gu.shape
    ff = w_dn.shape[2]
    nb = n_rows // tb
    live = lambda i, nu: jnp.minimum(i, nu[0] - 1)
    grid_spec = pltpu.PrefetchScalarGridSpec(
        num_scalar_prefetch=2,
        grid=(nb,),
        in_specs=[pl.BlockSpec((tb, half), lambda i, be, nu: (live(i, nu), 0)),
                  pl.BlockSpec((1, 1, d, ff2), lambda i, be, nu: (layer, be[live(i, nu)], 0, 0)),
                  pl.BlockSpec((1, 1, ff, d), lambda i, be, nu: (layer, be[live(i, nu)], 0, 0))],
        out_specs=pl.BlockSpec((tb, half), lambda i, be, nu: (live(i, nu), 0)),
        scratch_shapes=[pltpu.VMEM((d, ff2), BF16), pltpu.VMEM((ff, d), BF16)])
    return pl.pallas_call(
        _expert_kernel,
        grid_spec=grid_spec,
        out_shape=jax.ShapeDtypeStruct((n_rows, half), U32),
        compiler_params=_cparams("arbitrary"),
        name="moe_expert_ffn",
    )(block_e, n_used, xs, w_gu, w_dn)


def _combine_kernel(dest_ref, wts_ref, ys_hbm, h_ref, x_ref, mod_ref, sgu_ref, sdn_ref, fn_ref,
                    o_ref, gbuf, sem, *, tc, final):
    def issue(t, carry):
        for k in range(TOP_K):
            _row_copy(ys_hbm, dest_ref[0, k, t], gbuf, k * tc + t, sem).start()
        return carry

    lax.fori_loop(0, tc, issue, 0)

    ff = sdn_ref.shape[0]
    gu = jnp.dot(_unpack_rows_bf16(h_ref[...]), sgu_ref[...], preferred_element_type=F32)
    act = (_silu(gu[:, :ff]) * gu[:, ff:]).astype(BF16)
    shared = jnp.dot(act, sdn_ref[...], preferred_element_type=F32)

    def drain(t, carry):
        for _ in range(TOP_K):
            _row_copy(ys_hbm, 0, gbuf, 0, sem).wait()
        return carry

    lax.fori_loop(0, tc, drain, 0)

    w = wts_ref[...]
    lo, hi = _unpack_rows(gbuf[0:tc, :])
    acc_lo, acc_hi = w[:, 0:1] * lo, w[:, 0:1] * hi
    for k in range(1, TOP_K):
        lo, hi = _unpack_rows(gbuf[k * tc:(k + 1) * tc, :])
        acc_lo = acc_lo + w[:, k:k + 1] * lo
        acc_hi = acc_hi + w[:, k:k + 1] * hi
    moe = jnp.concatenate([acc_lo, acc_hi], axis=1) + shared
    x2 = x_ref[...] + mod_ref[0][5:6] * moe
    if final:
        x2 = x2 * lax.rsqrt(jnp.mean(x2 * x2, axis=-1, keepdims=True) + NORM_EPS) * fn_ref[...]
    o_ref[...] = x2


def _combine(dest3, wts_t, ys, h2p, x1, modb, s_gu, s_dn, final_norm, seq, final):
    n, d = x1.shape
    nt, _, tc = dest3.shape
    half = d // 2
    tiles_per_seq = seq // tc
    full2 = lambda i: (0, 0)
    return pl.pallas_call(
        functools.partial(_combine_kernel, tc=tc, final=final),
        grid=(nt,),
        in_specs=[pl.BlockSpec((1, TOP_K, tc), lambda i: (i, 0, 0), memory_space=pltpu.SMEM),
                  pl.BlockSpec((tc, TOP_K), lambda i: (i, 0)),
                  pl.BlockSpec(memory_space=pl.ANY),
                  pl.BlockSpec((tc, half), lambda i: (i, 0)),
                  pl.BlockSpec((tc, d), lambda i: (i, 0)),
                  pl.BlockSpec((1, MOD_ROWS, d), lambda i: (i // tiles_per_seq, 0, 0)),
                  pl.BlockSpec(s_gu.shape, full2),
                  pl.BlockSpec(s_dn.shape, full2),
                  pl.BlockSpec((1, d), full2)],
        out_specs=pl.BlockSpec((tc, d), lambda i: (i, 0)),
        out_shape=jax.ShapeDtypeStruct((n, d), F32),
        scratch_shapes=[pltpu.VMEM((TOP_K * tc, half), U32), pltpu.SemaphoreType.DMA(())],
        compiler_params=_cparams("arbitrary"),
        name="moe_combine",
    )(dest3, wts_t, ys, h2p, x1, modb, s_gu, s_dn, final_norm.reshape(1, d))


def _moe(x1, h2p, logits_t, modb, bias_perm, w_gu, w_dn, layer, s_gu, s_dn, final_norm, seq,
         final):
    n, d = x1.shape
    ne = w_gu.shape[1]
    tb = 256
    tok_tile = _tile(seq, 256)
    n_blocks = n * TOP_K // tb + ne
    eidx, wts = _routing(logits_t, bias_perm)
    dest, block_e, n_used = _dispatch_plan(eidx, ne, tb, n_blocks)
    dest3 = dest.reshape(TOP_K, n // tok_tile, tok_tile).transpose(1, 0, 2)
    xs = _dispatch(dest3, h2p, n_blocks * tb)
    ys = _expert_ffn(xs, block_e[0, :n_blocks], n_used[0, :1], w_gu, w_dn, layer, tb)
    return _combine(dest3, wts.T, ys, h2p, x1, modb, s_gu, s_dn, final_norm, seq, final)


def _expert_major_rows(a):
    per = a.shape[0] // N_GROUPS
    return a.reshape((N_GROUPS, per) + a.shape[1:]).swapaxes(0, 1).reshape(a.shape)


def _pairs_to_halves(w, heads):
    d, cols = w.shape
    dk = cols // heads
    return w.reshape(d, heads, dk // 2, 2).transpose(0, 1, 3, 2).reshape(d, cols)


def kernel(x, c, positions, mod_w, mod_b, norm_mix, norm_ffn, ret_w_in, ret_w_out, ret_out_gain, conv_w_in, conv_dw_w, conv_dw_b, conv_ln_g, conv_ln_b, conv_w_out, router_w, router_bias, exp_w_gu, exp_w_down, shared_w_gu, shared_w_down, final_norm):
    batch, seq, d = x.shape
    n = batch * seq
    depth = mod_w.shape[0]
    heads = RET_HEADS
    qk_cols = d
    mods = _modulation(c, mod_w, mod_b)
    xt = x.reshape(n, d)
    for i in range(depth):
        modb = _mod_block(mods[i], d)
        g_mix = norm_mix[i].reshape(1, d)
        g_ffn = norm_ffn[i].reshape(1, d)
        j = i // 2
        if i % 2 == 0:
            w_in = ret_w_in[j]
            w_in = jnp.concatenate([_pairs_to_halves(w_in[:, :qk_cols], heads),
                                    _pairs_to_halves(w_in[:, qk_cols:2 * qk_cols], heads),
                                    w_in[:, 2 * qk_cols:]], axis=1).astype(BF16)
            proj = _normmod_proj(xt, g_mix, modb, w_in, seq, glu=False)
            y = _retention_core(proj, positions, ret_out_gain[j], batch, seq, d)
            w_out = ret_w_out[j].astype(BF16)
        else:
            u = _normmod_proj(xt, g_mix, modb, conv_w_in[j].astype(BF16), seq, glu=True)
            y = _conv_ln_silu(u, conv_dw_w[j], conv_dw_b[j], conv_ln_g[j], conv_ln_b[j], batch, seq)
            w_out = conv_w_out[j].astype(BF16)
        router_wt = _expert_major_rows(router_w[i].T)
        x1, h2p, logits_t = _out_projection(y, w_out, xt, modb, g_ffn, router_wt, seq)
        xt = _moe(x1, h2p, logits_t, modb, _expert_major_rows(router_bias[i]),
                  exp_w_gu, exp_w_down, i,
                  shared_w_gu[i].astype(BF16), shared_w_down[i].astype(BF16),
                  final_norm, seq, final=(i == depth - 1))
    return xt.reshape(batch, seq, d)
```

```python
import functools

import jax
import jax.numpy as jnp
from jax import lax
from jax.experimental import pallas as pl
from jax.experimental.pallas import tpu as pltpu

F32 = jnp.float32
BF16 = jnp.bfloat16
I32 = jnp.int32
U32 = jnp.uint32

RET_HEADS = 8
ROPE_BASE = 10000.0
CONV_WIDTH = 31
CONV_HALO = 32
N_GROUPS = 8
TOPK_GROUPS = 4
TOP_K = 8
ROUTED_SCALE = 2.5
NORM_EPS = 1e-6
MOD_ROWS = 8
LANES = 128
SUBLANES = 8
ROW_SUB = 8
EXPERT_BLOCK_ROWS = 512
VMEM_LIMIT = 56 * 1024 * 1024


def _cparams(*sem):
    return pltpu.CompilerParams(dimension_semantics=sem, vmem_limit_bytes=VMEM_LIMIT)


def _tile(n, pref):
    t = min(n, pref)
    assert n % t == 0, (n, pref)
    return t


def _split_bf16(a):
    hi = a.astype(BF16)
    lo = (a - hi.astype(F32)).astype(BF16)
    return hi, lo


def _dot3(a, b, dims):
    ah, al = _split_bf16(a)
    bh, bl = _split_bf16(b)
    dg = functools.partial(lax.dot_general, dimension_numbers=dims, preferred_element_type=F32)
    return dg(ah, bh) + dg(ah, bl) + dg(al, bh)


_NN = (((1,), (0,)), ((), ()))
_NT = (((1,), (1,)), ((), ()))
_TN = (((0,), (0,)), ((), ()))


def _normmod(x, g, shift, scale):
    y = x * lax.rsqrt(jnp.mean(x * x, axis=-1, keepdims=True) + NORM_EPS)
    return (y * g) * (1.0 + scale) + shift


def _silu(x):
    return x * jax.nn.sigmoid(x)


def _pack_rows(h):
    half = h.shape[1] // 2
    bits = lax.bitcast_convert_type(h.astype(BF16).astype(F32), U32)
    lo = lax.shift_right_logical(bits[:, :half], jnp.uint32(16))
    hi = bits[:, half:] & jnp.uint32(0xFFFF0000)
    return hi | lo


def _unpack_rows(w):
    lo = lax.bitcast_convert_type(lax.shift_left(w, jnp.uint32(16)), F32)
    hi = lax.bitcast_convert_type(w & jnp.uint32(0xFFFF0000), F32)
    return lo, hi


def _store_row_tiles(ref, base, packed):
    t = packed.shape[0]
    assert packed.shape[1] == ROW_SUB * LANES
    for s in range(ROW_SUB):
        ref[pl.ds(base + s, t, stride=ROW_SUB), :] = packed[:, s * LANES:(s + 1) * LANES]


def _load_row_tiles(ref, base, t):
    parts = [_unpack_rows(ref[pl.ds(base + s, t, stride=ROW_SUB), :]) for s in range(ROW_SUB)]
    lo = jnp.concatenate([p[0] for p in parts], axis=1)
    hi = jnp.concatenate([p[1] for p in parts], axis=1)
    return lo, hi


def _load_row_tiles_bf16(ref, base, t):
    lo, hi = _load_row_tiles(ref, base, t)
    return jnp.concatenate([lo.astype(BF16), hi.astype(BF16)], axis=1)


def _mod_kernel(c_ref, w_ref, b_ref, o_ref):
    c = c_ref[...]
    o_ref[0] = _dot3(_silu(c), w_ref[0], _NN) + b_ref[0]


def _modulation(c, mod_w, mod_b):
    depth, d, n6 = mod_w.shape
    b = c.shape[0]
    assert b <= MOD_ROWS
    c_pad = jnp.zeros((MOD_ROWS, d), F32).at[:b].set(c)
    tn = _tile(n6, 1024)
    out = pl.pallas_call(
        _mod_kernel,
        grid=(depth, n6 // tn),
        in_specs=[pl.BlockSpec((MOD_ROWS, d), lambda i, j: (0, 0)),
                  pl.BlockSpec((1, d, tn), lambda i, j: (i, 0, j)),
                  pl.BlockSpec((1, 1, tn), lambda i, j: (i, 0, j))],
        out_specs=pl.BlockSpec((1, MOD_ROWS, tn), lambda i, j: (i, 0, j)),
        out_shape=jax.ShapeDtypeStruct((depth, MOD_ROWS, n6), F32),
        compiler_params=_cparams("arbitrary", "arbitrary"),
        name="adaln_modulation",
    )(c_pad, mod_w, mod_b.reshape(depth, 1, n6))
    return out[:, :b]


def _mod_block(mod_i, d):
    b = mod_i.shape[0]
    m = mod_i.reshape(b, 6, d)
    return jnp.concatenate([m, jnp.zeros((b, MOD_ROWS - 6, d), F32)], axis=1)


def _proj_kernel(x_ref, g_ref, mod_ref, w_ref, o_ref, h_scr):
    @pl.when(pl.program_id(1) == 0)
    def _():
        m = mod_ref[0]
        h_scr[...] = _normmod(x_ref[...], g_ref[...], m[0:1], m[1:2]).astype(BF16)

    o_ref[...] = jnp.dot(h_scr[...], w_ref[...], preferred_element_type=F32).astype(o_ref.dtype)


def _glu_proj_kernel(x_ref, g_ref, mod_ref, wa_ref, wb_ref, o_ref, h_scr):
    @pl.when(pl.program_id(1) == 0)
    def _():
        m = mod_ref[0]
        h_scr[...] = _normmod(x_ref[...], g_ref[...], m[0:1], m[1:2]).astype(BF16)

    h = h_scr[...]
    a = jnp.dot(h, wa_ref[...], preferred_element_type=F32)
    b = jnp.dot(h, wb_ref[...], preferred_element_type=F32)
    o_ref[...] = a * jax.nn.sigmoid(b)


def _normmod_proj(x, g, modb, w, seq, glu):
    n, d = x.shape
    nout = w.shape[1] // 2 if glu else w.shape[1]
    tm = _tile(seq, 1024)
    tn = _tile(nout, 512 if glu else 1024)
    tiles_per_seq = seq // tm
    x_spec = pl.BlockSpec((tm, d), lambda i, j: (i, 0))
    g_spec = pl.BlockSpec((1, d), lambda i, j: (0, 0))
    m_spec = pl.BlockSpec((1, MOD_ROWS, d), lambda i, j: (i // tiles_per_seq, 0, 0))
    if glu:
        half_blocks = nout // tn
        in_specs = [x_spec, g_spec, m_spec,
                    pl.BlockSpec((d, tn), lambda i, j: (0, j)),
                    pl.BlockSpec((d, tn), lambda i, j: (0, j + half_blocks))]
        body, args, odt = _glu_proj_kernel, (x, g, modb, w, w), F32
    else:
        in_specs = [x_spec, g_spec, m_spec, pl.BlockSpec((d, tn), lambda i, j: (0, j))]
        body, args, odt = _proj_kernel, (x, g, modb, w), BF16
    return pl.pallas_call(
        body,
        grid=(n // tm, nout // tn),
        in_specs=in_specs,
        out_specs=pl.BlockSpec((tm, tn), lambda i, j: (i, j)),
        out_shape=jax.ShapeDtypeStruct((n, nout), odt),
        scratch_shapes=[pltpu.VMEM((tm, d), BF16)],
        compiler_params=_cparams("arbitrary", "arbitrary"),
        name="glu_in_projection" if glu else "in_projection",
    )(*args)


def _retention_kernel(pos_ref, inv_ref, q_ref, k_ref, v_ref, g_ref, intra_ref, qd_ref, kd_ref,
                      cd_ref, gain_ref, o_ref, state, *, dk, dv):
    @pl.when(pl.program_id(1) == 0)
    def _():
        state[...] = jnp.zeros_like(state)

    half = dk // 2
    ang = pos_ref[...].astype(F32) * inv_ref[...]
    cos = jnp.cos(ang)
    sin = jnp.sin(ang)

    def rot(ref, h):
        x1 = ref[:, h * dk:h * dk + half].astype(F32)
        x2 = ref[:, h * dk + half:(h + 1) * dk].astype(F32)
        return jnp.concatenate([x1 * cos - x2 * sin, x1 * sin + x2 * cos], axis=1)

    for h in range(RET_HEADS):
        q = rot(q_ref, h)
        k = rot(k_ref, h) * (dk ** -0.5)
        v = v_ref[:, h * dv:(h + 1) * dv]
        s = lax.dot_general(q.astype(BF16), k.astype(BF16), _NT, preferred_element_type=F32)
        p = (s * intra_ref[h]).astype(BF16)
        inner = jnp.dot(p, v, preferred_element_type=F32)
        st = state[h]
        cross = jnp.dot((q * qd_ref[h]).astype(BF16), st.astype(BF16),
                        preferred_element_type=F32)
        kv = lax.dot_general((k * kd_ref[h]).astype(BF16), v, _TN, preferred_element_type=F32)
        state[h] = cd_ref[h][:, :1] * st + kv
        o = inner + cross
        o = o * lax.rsqrt(jnp.mean(o * o, axis=-1, keepdims=True) + NORM_EPS)
        gate = g_ref[:, h * dv:(h + 1) * dv].astype(F32)
        o_ref[:, h * dv:(h + 1) * dv] = (_silu(gate) * (o * gain_ref[:, h * dv:(h + 1) * dv])
                                         ).astype(o_ref.dtype)


def _retention_core(proj, positions, out_gain, batch, seq, d):
    n = batch * seq
    heads, dk = RET_HEADS, d // RET_HEADS
    dv = 2 * dk
    c = _tile(seq, 128)
    nc = seq // c
    half = dk // 2
    inv = (1.0 / (ROPE_BASE ** jnp.linspace(0.0, 1.0, half, dtype=F32))).reshape(1, half)
    log_gamma = jnp.log1p(-jnp.exp2(-5.0 - jnp.arange(heads, dtype=F32)))
    idx = jnp.arange(c, dtype=F32)
    rel = idx[:, None] - idx[None, :]
    intra = jnp.where(rel >= 0, jnp.exp(log_gamma[:, None, None] * jnp.maximum(rel, 0.0)), 0.0)
    q_decay = jnp.exp(log_gamma[:, None] * (idx + 1.0))[:, :, None]
    k_decay = jnp.exp(log_gamma[:, None] * (c - 1.0 - idx))[:, :, None]
    chunk_decay = jnp.broadcast_to(jnp.exp(log_gamma * c)[:, None, None], (heads, 1, LANES))
    row = lambda b, j: b * nc + j
    full3 = lambda b, j: (0, 0, 0)
    return pl.pallas_call(
        functools.partial(_retention_kernel, dk=dk, dv=dv),
        grid=(batch, nc),
        in_specs=[pl.BlockSpec((c, 1), lambda b, j: (row(b, j), 0)),
                  pl.BlockSpec((1, half), lambda b, j: (0, 0)),
                  pl.BlockSpec((c, heads * dk), lambda b, j: (row(b, j), 0)),
                  pl.BlockSpec((c, heads * dk), lambda b, j: (row(b, j), 1)),
                  pl.BlockSpec((c, heads * dv), lambda b, j: (row(b, j), 1)),
                  pl.BlockSpec((c, heads * dv), lambda b, j: (row(b, j), 2)),
                  pl.BlockSpec((heads, c, c), full3),
                  pl.BlockSpec((heads, c, 1), full3),
                  pl.BlockSpec((heads, c, 1), full3),
                  pl.BlockSpec((heads, 1, LANES), full3),
                  pl.BlockSpec((1, heads * dv), lambda b, j: (0, 0))],
        out_specs=pl.BlockSpec((c, heads * dv), lambda b, j: (row(b, j), 0)),
        out_shape=jax.ShapeDtypeStruct((n, heads * dv), BF16),
        scratch_shapes=[pltpu.VMEM((heads, dk, dv), F32)],
        compiler_params=_cparams("arbitrary", "arbitrary"),
        name="retention_core",
    )(positions.reshape(n, 1), inv, proj, proj, proj, proj, intra, q_decay, k_decay,
      chunk_decay, out_gain.reshape(1, heads * dv))


def _conv_kernel(u_ref, halo_ref, w_ref, b_ref, lg_ref, lb_ref, o_ref, ext, conv, *, tile):
    first = pl.program_id(1) == 0
    ext[0, 0:CONV_HALO, :] = jnp.where(first, 0.0, halo_ref[...])
    ext[0, CONV_HALO:CONV_HALO + tile, :] = u_ref[...]
    span = ext.shape[1] - SUBLANES
    for s in range(1, SUBLANES):
        ext[s, 0:span, :] = ext[0, s:s + span, :]
    base = CONV_HALO - (CONV_WIDTH - 1)
    ch = u_ref.shape[1]
    cw = min(ch, 4 * LANES)
    for c0 in range(0, ch, cw):
        part = jnp.zeros((tile, cw), F32) + b_ref[:, c0:c0 + cw]
        for j in range(CONV_WIDTH):
            shift, start = (base + j) % SUBLANES, (base + j) // SUBLANES * SUBLANES
            part = part + ext[shift, start:start + tile, c0:c0 + cw] * w_ref[j:j + 1, c0:c0 + cw]
        conv[:, c0:c0 + cw] = part
    acc = conv[...]
    mu = jnp.mean(acc, axis=-1, keepdims=True)
    cen = acc - mu
    var = jnp.mean(cen * cen, axis=-1, keepdims=True)
    y = cen * lax.rsqrt(var + NORM_EPS) * lg_ref[...] + lb_ref[...]
    o_ref[...] = _silu(y).astype(o_ref.dtype)


def _conv_ln_silu(u, dw_w, dw_b, ln_g, ln_b, batch, seq):
    n, ch = u.shape
    t = _tile(seq, 64)
    assert t % CONV_HALO == 0
    nt = seq // t
    halo_per_tile = t // CONV_HALO
    row = lambda b, j: b * nt + j
    vec = pl.BlockSpec((1, ch), lambda b, j: (0, 0))
    return pl.pallas_call(
        functools.partial(_conv_kernel, tile=t),
        grid=(batch, nt),
        in_specs=[pl.BlockSpec((t, ch), lambda b, j: (row(b, j), 0)),
                  pl.BlockSpec((CONV_HALO, ch),
                               lambda b, j: (jnp.maximum(row(b, j) * halo_per_tile - 1, 0), 0)),
                  pl.BlockSpec((CONV_WIDTH, ch), lambda b, j: (0, 0)),
                  vec, vec, vec],
        out_specs=pl.BlockSpec((t, ch), lambda b, j: (row(b, j), 0)),
        out_shape=jax.ShapeDtypeStruct((n, ch), BF16),
        scratch_shapes=[pltpu.VMEM((SUBLANES, CONV_HALO + t, ch), F32),
                        pltpu.VMEM((t, ch), F32)],
        compiler_params=_cparams("arbitrary", "arbitrary"),
        name="conv_ln_silu",
    )(u, u, dw_w, dw_b.reshape(1, ch), ln_g.reshape(1, ch), ln_b.reshape(1, ch))


def _outproj_kernel(y_ref, w_ref, x_ref, mod_ref, g_ref, rw_ref, x1_ref, h2_ref, lg_ref, acc):
    k = pl.program_id(1)

    @pl.when(k == 0)
    def _():
        acc[...] = jnp.zeros_like(acc)

    acc[...] += jnp.dot(y_ref[...], w_ref[...], preferred_element_type=F32)

    @pl.when(k == pl.num_programs(1) - 1)
    def _():
        m = mod_ref[0]
        x1 = x_ref[...] + m[2:3] * acc[...]
        x1_ref[...] = x1
        h2 = _normmod(x1, g_ref[...], m[3:4], m[4:5])
        _store_row_tiles(h2_ref, 0, _pack_rows(h2))
        lg_ref[...] = _dot3(rw_ref[...], h2, _NT)


def _out_projection(y, w, x, modb, g_ffn, router_wt, seq):
    n, kdim = y.shape
    d = x.shape[1]
    ne = router_wt.shape[0]
    assert d // 2 == ROW_SUB * LANES
    tm = _tile(seq, 512)
    tk = _tile(kdim, 1024)
    tiles_per_seq = seq // tm
    return pl.pallas_call(
        _outproj_kernel,
        grid=(n // tm, kdim // tk),
        in_specs=[pl.BlockSpec((tm, tk), lambda i, k: (i, k)),
                  pl.BlockSpec((tk, d), lambda i, k: (k, 0)),
                  pl.BlockSpec((tm, d), lambda i, k: (i, 0)),
                  pl.BlockSpec((1, MOD_ROWS, d), lambda i, k: (i // tiles_per_seq, 0, 0)),
                  pl.BlockSpec((1, d), lambda i, k: (0, 0)),
                  pl.BlockSpec((ne, d), lambda i, k: (0, 0))],
        out_specs=[pl.BlockSpec((tm, d), lambda i, k: (i, 0)),
                   pl.BlockSpec((tm * ROW_SUB, LANES), lambda i, k: (i, 0)),
                   pl.BlockSpec((ne, tm), lambda i, k: (0, i))],
        out_shape=[jax.ShapeDtypeStruct((n, d), F32),
                   jax.ShapeDtypeStruct((n * ROW_SUB, LANES), U32),
                   jax.ShapeDtypeStruct((ne, n), F32)],
        scratch_shapes=[pltpu.VMEM((tm, d), F32)],
        compiler_params=_cparams("arbitrary", "arbitrary"),
        name="out_projection",
    )(y, w, x, modb, g_ffn, router_wt)


def _routing_kernel(lg_ref, bias_ref, eidx_ref, wts_ref):
    neg = -jnp.inf
    per = N_GROUPS
    tr = lg_ref.shape[1]
    scores = jax.nn.sigmoid(lg_ref[...])
    biased = scores + bias_ref[...]
    n_slab = lg_ref.shape[0] // per
    s_j = [scores[per * j:per * (j + 1), :] for j in range(n_slab)]
    b_j = [biased[per * j:per * (j + 1), :] for j in range(n_slab)]

    m1 = functools.reduce(jnp.maximum, b_j)
    ties = functools.reduce(lambda a, b: a + b, [(b == m1).astype(F32) for b in b_j])
    below = functools.reduce(jnp.maximum, [jnp.where(b < m1, b, neg) for b in b_j])
    grp = m1 + jnp.where(ties >= 2.0, m1, below)

    gid = lax.broadcasted_iota(I32, (per, tr), 0).astype(F32)
    chosen = jnp.zeros((per, tr), F32)
    cur = grp
    for _ in range(TOPK_GROUPS):
        mx = jnp.max(cur, axis=0, keepdims=True)
        first = jnp.min(jnp.where(cur == mx, gid, float(per)), axis=0, keepdims=True)
        pick = gid == first
        chosen = jnp.where(pick, 1.0, chosen)
        cur = jnp.where(pick, neg, cur)

    eid_j = [gid * float(n_slab) + float(j) for j in range(n_slab)]
    cur_j = [jnp.where(chosen > 0.0, b, neg) for b in b_j]
    picked_scores = []
    for k in range(TOP_K):
        mx = jnp.max(functools.reduce(jnp.maximum, cur_j), axis=0, keepdims=True)
        cand = functools.reduce(
            jnp.minimum, [jnp.where(c == mx, e, float(per * n_slab)) for c, e in zip(cur_j, eid_j)])
        first = jnp.min(cand, axis=0, keepdims=True)
        pick_j = [e == first for e in eid_j]
        sc = functools.reduce(lambda a, b: a + b,
                              [jnp.where(p, s, 0.0) for p, s in zip(pick_j, s_j)])
        picked_scores.append(jnp.sum(sc, axis=0, keepdims=True))
        cur_j = [jnp.where(p, neg, c) for p, c in zip(pick_j, cur_j)]
        eidx_ref[k:k + 1, :] = first.astype(I32)
    total = functools.reduce(lambda a, b: a + b, picked_scores)
    for k in range(TOP_K):
        wts_ref[k:k + 1, :] = picked_scores[k] / total * ROUTED_SCALE


def _routing(logits_t, bias_perm):
    ne, n = logits_t.shape
    tr = _tile(n, 1024)
    return pl.pallas_call(
        _routing_kernel,
        grid=(n // tr,),
        in_specs=[pl.BlockSpec((ne, tr), lambda i: (0, i)),
                  pl.BlockSpec((ne, 1), lambda i: (0, 0))],
        out_specs=[pl.BlockSpec((TOP_K, tr), lambda i: (0, i)),
                   pl.BlockSpec((TOP_K, tr), lambda i: (0, i))],
        out_shape=[jax.ShapeDtypeStruct((TOP_K, n), I32),
                   jax.ShapeDtypeStruct((TOP_K, n), F32)],
        compiler_params=_cparams("arbitrary"),
        name="moe_routing",
    )(logits_t, bias_perm.reshape(ne, 1))


def _plan_kernel(e_ref, u_ref, l_ref, dest_ref, be_ref, nu_ref, ss_ref, sn_ref, cnt, base, carry,
                 *, tb, ne):
    p = pl.program_id(0)
    t = pl.program_id(1)
    tp = e_ref.shape[1]
    ei = e_ref[...]
    eid = lax.broadcasted_iota(I32, (ne, tp), 0)
    hit = [ei[k:k + 1, :] == eid for k in range(TOP_K)]
    onehot = functools.reduce(lambda a, b: a + b, [h.astype(F32) for h in hit])
    tile_cnt = jnp.sum(onehot, axis=1, keepdims=True)

    @pl.when((p == 0) & (t == 0))
    def _():
        cnt[...] = jnp.zeros_like(cnt)

    @pl.when(p == 0)
    def _():
        cnt[...] += tile_cnt

    @pl.when((p == 1) & (t == 0))
    def _():
        nblk = jnp.floor((cnt[...] + float(tb - 1)) * (1.0 / tb))
        start_blk = jnp.dot(l_ref[...], nblk.astype(BF16), preferred_element_type=F32)
        base[...] = start_blk * float(tb)
        carry[...] = jnp.zeros_like(carry)
        end_blk = start_blk + nblk
        nbp = be_ref.shape[1]
        blk = lax.broadcasted_iota(I32, (ne, nbp), 1).astype(F32)
        owner = jnp.sum((end_blk[:, :1] <= blk).astype(F32), axis=0, keepdims=True)
        be_ref[...] = jnp.minimum(owner, float(ne - 1)).astype(I32)
        nu_ref[...] = end_blk[ne - 1:ne, :].astype(I32)
        diag = (lax.broadcasted_iota(I32, (ne, LANES), 0)
                == lax.broadcasted_iota(I32, (ne, LANES), 1))
        ss_ref[...] = jnp.sum(jnp.where(diag, start_blk, 0.0), axis=0, keepdims=True).astype(I32)
        sn_ref[...] = jnp.sum(jnp.where(diag, nblk, 0.0), axis=0, keepdims=True).astype(I32)

    @pl.when(p == 1)
    def _():
        before = jnp.dot(onehot.astype(BF16), u_ref[...], preferred_element_type=F32)
        rowpos = base[:, :1] + carry[:, :1] + before
        for k in range(TOP_K):
            dest_ref[k:k + 1, :] = jnp.sum(jnp.where(hit[k], rowpos, 0.0), axis=0,
                                           keepdims=True).astype(I32)
        carry[...] += tile_cnt


def _dispatch_plan(eidx, ne, tb, n_blocks):
    n = eidx.shape[1]
    tp = _tile(n, 512)
    assert n // tb + 1 <= 256
    assert ne <= LANES
    lane_row = pl.BlockSpec((1, LANES), lambda p, t: (0, 0))
    nbp = -(-n_blocks // LANES) * LANES
    upper = (jnp.arange(tp)[:, None] < jnp.arange(tp)[None, :]).astype(BF16)
    lower = (jnp.arange(ne)[None, :] < jnp.arange(ne)[:, None]).astype(BF16)
    return pl.pallas_call(
        functools.partial(_plan_kernel, tb=tb, ne=ne),
        grid=(2, n // tp),
        in_specs=[pl.BlockSpec((TOP_K, tp), lambda p, t: (0, t)),
                  pl.BlockSpec((tp, tp), lambda p, t: (0, 0)),
                  pl.BlockSpec((ne, ne), lambda p, t: (0, 0))],
        out_specs=[pl.BlockSpec((TOP_K, tp), lambda p, t: (0, t * p)),
                   pl.BlockSpec((1, nbp), lambda p, t: (0, 0)),
                   lane_row, lane_row, lane_row],
        out_shape=[jax.ShapeDtypeStruct((TOP_K, n), I32),
                   jax.ShapeDtypeStruct((1, nbp), I32)]
        + [jax.ShapeDtypeStruct((1, LANES), I32)] * 3,
        scratch_shapes=[pltpu.VMEM((ne, LANES), F32)] * 3,
        compiler_params=_cparams("arbitrary", "arbitrary"),
        name="moe_dispatch_plan",
    )(eidx, upper, lower)


def _rows(ref, row, count=1):
    return ref.at[pl.ds(pl.multiple_of(row * ROW_SUB, ROW_SUB), count * ROW_SUB), :]


def _row_copy(src, s_row, dst, d_row, sem):
    return pltpu.make_async_copy(_rows(src, s_row), _rows(dst, d_row), sem)


def _dispatch_kernel(ss_ref, sn_ref, dest_ref, h_ref, xs_hbm, zeros, sem, zsem, *, td, tb, ne):
    @pl.when(pl.program_id(0) == 0)
    def _():
        zeros[...] = jnp.zeros_like(zeros)

        def fill_copy(blk):
            return pltpu.make_async_copy(zeros, _rows(xs_hbm, blk * tb, tb), zsem)

        def fill(e, carry):
            @pl.when(sn_ref[e] > 0)
            def _():
                fill_copy(ss_ref[e] + sn_ref[e] - 1).start()
            return carry

        def filled(e, carry):
            @pl.when(sn_ref[e] > 0)
            def _():
                fill_copy(0).wait()
            return carry

        lax.fori_loop(0, ne, fill, 0)
        lax.fori_loop(0, ne, filled, 0)

    def issue(t, carry):
        for k in range(TOP_K):
            _row_copy(h_ref, t, xs_hbm, dest_ref[0, k, t], sem).start()
        return carry

    lax.fori_loop(0, td, issue, 0)

    def drain(t, carry):
        for _ in range(TOP_K):
            _row_copy(h_ref, 0, xs_hbm, 0, sem).wait()
        return carry

    lax.fori_loop(0, td, drain, 0)


def _dispatch(dest3, h2p, seg_start, seg_nblk, n_rows, tb, ne):
    nt, _, td = dest3.shape
    grid_spec = pltpu.PrefetchScalarGridSpec(
        num_scalar_prefetch=2,
        grid=(nt,),
        in_specs=[pl.BlockSpec((1, TOP_K, td), lambda i, ss, sn: (i, 0, 0),
                               memory_space=pltpu.SMEM),
                  pl.BlockSpec((td * ROW_SUB, LANES), lambda i, ss, sn: (i, 0))],
        out_specs=pl.BlockSpec(memory_space=pl.ANY),
        scratch_shapes=[pltpu.VMEM((tb * ROW_SUB, LANES), U32),
                        pltpu.SemaphoreType.DMA(()), pltpu.SemaphoreType.DMA(())])
    return pl.pallas_call(
        functools.partial(_dispatch_kernel, td=td, tb=tb, ne=ne),
        grid_spec=grid_spec,
        out_shape=jax.ShapeDtypeStruct((n_rows * ROW_SUB, LANES), U32),
        compiler_params=_cparams("arbitrary"),
        name="moe_row_dispatch",
    )(seg_start, seg_nblk, dest3, h2p)


def _expert_kernel(be_ref, nu_ref, x_ref, wgu_ref, wdn_ref, o_ref, wgu_b, wdn_b):
    i = pl.program_id(0)
    live = i < nu_ref[0]
    new_expert = (i == 0) | (be_ref[i] != be_ref[jnp.maximum(i - 1, 0)])

    @pl.when(live & new_expert)
    def _():
        wgu_b[...] = wgu_ref[0, 0].astype(BF16)
        wdn_b[...] = wdn_ref[0, 0].astype(BF16)

    @pl.when(live)
    def _():
        ff = wdn_b.shape[0]
        tb = x_ref.shape[0] // ROW_SUB
        x = _load_row_tiles_bf16(x_ref, 0, tb)
        gu = jnp.dot(x, wgu_b[...], preferred_element_type=F32)
        act = (_silu(gu[:, :ff]) * gu[:, ff:]).astype(BF16)
        _store_row_tiles(o_ref, 0, _pack_rows(jnp.dot(act, wdn_b[...], preferred_element_type=F32)))


def _expert_ffn(xs, block_e, n_used, w_gu, w_dn, layer, tb):
    _, _, d, ff2 = w_gu.shape
    ff = w_dn.shape[2]
    nb = xs.shape[0] // (tb * ROW_SUB)
    live = lambda i, nu: jnp.minimum(i, nu[0] - 1)
    row_block = pl.BlockSpec((tb * ROW_SUB, LANES), lambda i, be, nu: (live(i, nu), 0))
    grid_spec = pltpu.PrefetchScalarGridSpec(
        num_scalar_prefetch=2,
        grid=(nb,),
        in_specs=[row_block,
                  pl.BlockSpec((1, 1, d, ff2), lambda i, be, nu: (layer, be[live(i, nu)], 0, 0)),
                  pl.BlockSpec((1, 1, ff, d), lambda i, be, nu: (layer, be[live(i, nu)], 0, 0))],
        out_specs=row_block,
        scratch_shapes=[pltpu.VMEM((d, ff2), BF16), pltpu.VMEM((ff, d), BF16)])
    return pl.pallas_call(
        _expert_kernel,
        grid_spec=grid_spec,
        out_shape=jax.ShapeDtypeStruct(xs.shape, U32),
        compiler_params=_cparams("arbitrary"),
        name="moe_expert_ffn",
    )(block_e, n_used, xs, w_gu, w_dn)


def _combine_kernel(dest_ref, next_ref, wts_ref, ys_hbm, h_ref, x_ref, mod_ref, sgu_ref, sdn_ref,
                    fn_ref, o_ref, gbuf, sem, *, tc, final):
    i = pl.program_id(0)
    slot_rows = TOP_K * tc

    def gather(d_ref, slot):
        def issue(t, carry):
            for k in range(TOP_K):
                _row_copy(ys_hbm, d_ref[0, k, t], gbuf, slot * slot_rows + k * tc + t,
                          sem.at[slot]).start()
            return carry

        lax.fori_loop(0, tc, issue, 0)

    @pl.when(i == 0)
    def _():
        gather(dest_ref, 0)

    @pl.when(i + 1 < pl.num_programs(0))
    def _():
        gather(next_ref, (i + 1) % 2)

    ff = sdn_ref.shape[0]
    gu = jnp.dot(_load_row_tiles_bf16(h_ref, 0, tc), sgu_ref[...], preferred_element_type=F32)
    act = (_silu(gu[:, :ff]) * gu[:, ff:]).astype(BF16)
    shared = jnp.dot(act, sdn_ref[...], preferred_element_type=F32)

    slot = i % 2

    def drain(t, carry):
        for _ in range(TOP_K):
            _row_copy(ys_hbm, 0, gbuf, 0, sem.at[slot]).wait()
        return carry

    lax.fori_loop(0, tc, drain, 0)

    w = wts_ref[...]
    base = slot * slot_rows * ROW_SUB
    lo, hi = _load_row_tiles(gbuf, base, tc)
    acc_lo, acc_hi = w[:, 0:1] * lo, w[:, 0:1] * hi
    for k in range(1, TOP_K):
        lo, hi = _load_row_tiles(gbuf, base + k * tc * ROW_SUB, tc)
        acc_lo = acc_lo + w[:, k:k + 1] * lo
        acc_hi = acc_hi + w[:, k:k + 1] * hi
    moe = jnp.concatenate([acc_lo, acc_hi], axis=1) + shared
    x2 = x_ref[...] + mod_ref[0][5:6] * moe
    if final:
        x2 = x2 * lax.rsqrt(jnp.mean(x2 * x2, axis=-1, keepdims=True) + NORM_EPS) * fn_ref[...]
    o_ref[...] = x2


def _combine(dest3, wts_t, ys, h2p, x1, modb, s_gu, s_dn, final_norm, seq, final):
    n, d = x1.shape
    nt, _, tc = dest3.shape
    tiles_per_seq = seq // tc
    full2 = lambda i: (0, 0)
    return pl.pallas_call(
        functools.partial(_combine_kernel, tc=tc, final=final),
        grid=(nt,),
        in_specs=[pl.BlockSpec((1, TOP_K, tc), lambda i: (i, 0, 0), memory_space=pltpu.SMEM),
                  pl.BlockSpec((1, TOP_K, tc), lambda i: (jnp.minimum(i + 1, nt - 1), 0, 0),
                               memory_space=pltpu.SMEM),
                  pl.BlockSpec((tc, TOP_K), lambda i: (i, 0)),
                  pl.BlockSpec(memory_space=pl.ANY),
                  pl.BlockSpec((tc * ROW_SUB, LANES), lambda i: (i, 0)),
                  pl.BlockSpec((tc, d), lambda i: (i, 0)),
                  pl.BlockSpec((1, MOD_ROWS, d), lambda i: (i // tiles_per_seq, 0, 0)),
                  pl.BlockSpec(s_gu.shape, full2),
                  pl.BlockSpec(s_dn.shape, full2),
                  pl.BlockSpec((1, d), full2)],
        out_specs=pl.BlockSpec((tc, d), lambda i: (i, 0)),
        out_shape=jax.ShapeDtypeStruct((n, d), F32),
        scratch_shapes=[pltpu.VMEM((2 * TOP_K * tc * ROW_SUB, LANES), U32),
                        pltpu.SemaphoreType.DMA((2,))],
        compiler_params=_cparams("arbitrary"),
        name="moe_combine",
    )(dest3, dest3, wts_t, ys, h2p, x1, modb, s_gu, s_dn, final_norm.reshape(1, d))


def _moe(x1, h2p, logits_t, modb, bias_perm, w_gu, w_dn, layer, s_gu, s_dn, final_norm, seq,
         final):
    n, d = x1.shape
    ne = w_gu.shape[1]
    tb = EXPERT_BLOCK_ROWS
    tok_tile = _tile(seq, 256)
    n_blocks = n * TOP_K // tb + ne
    eidx, wts = _routing(logits_t, bias_perm)
    dest, block_e, n_used, seg_start, seg_nblk = _dispatch_plan(eidx, ne, tb, n_blocks)
    dest3 = dest.reshape(TOP_K, n // tok_tile, tok_tile).transpose(1, 0, 2)
    xs = _dispatch(dest3, h2p, seg_start[0, :ne], seg_nblk[0, :ne], n_blocks * tb, tb, ne)
    ys = _expert_ffn(xs, block_e[0, :n_blocks], n_used[0, :1], w_gu, w_dn, layer, tb)
    return _combine(dest3, wts.T, ys, h2p, x1, modb, s_gu, s_dn, final_norm, seq, final)


def _expert_major_rows(a):
    per = a.shape[0] // N_GROUPS
    return a.reshape((N_GROUPS, per) + a.shape[1:]).swapaxes(0, 1).reshape(a.shape)


def _pairs_to_halves(w, heads):
    d, cols = w.shape
    dk = cols // heads
    return w.reshape(d, heads, dk // 2, 2).transpose(0, 1, 3, 2).reshape(d, cols)


def kernel(x, c, positions, mod_w, mod_b, norm_mix, norm_ffn, ret_w_in, ret_w_out, ret_out_gain, conv_w_in, conv_dw_w, conv_dw_b, conv_ln_g, conv_ln_b, conv_w_out, router_w, router_bias, exp_w_gu, exp_w_down, shared_w_gu, shared_w_down, final_norm):
    batch, seq, d = x.shape
    n = batch * seq
    depth = mod_w.shape[0]
    heads = RET_HEADS
    qk_cols = d
    mods = _modulation(c, mod_w, mod_b)
    xt = x.reshape(n, d)
    for i in range(depth):
        modb = _mod_block(mods[i], d)
        g_mix = norm_mix[i].reshape(1, d)
        g_ffn = norm_ffn[i].reshape(1, d)
        j = i // 2
        if i % 2 == 0:
            w_in = ret_w_in[j]
            w_in = jnp.concatenate([_pairs_to_halves(w_in[:, :qk_cols], heads),
                                    _pairs_to_halves(w_in[:, qk_cols:2 * qk_cols], heads),
                                    w_in[:, 2 * qk_cols:]], axis=1).astype(BF16)
            proj = _normmod_proj(xt, g_mix, modb, w_in, seq, glu=False)
            y = _retention_core(proj, positions, ret_out_gain[j], batch, seq, d)
            w_out = ret_w_out[j].astype(BF16)
        else:
            u = _normmod_proj(xt, g_mix, modb, conv_w_in[j].astype(BF16), seq, glu=True)
            y = _conv_ln_silu(u, conv_dw_w[j], conv_dw_b[j], conv_ln_g[j], conv_ln_b[j], batch, seq)
            w_out = conv_w_out[j].astype(BF16)
        router_wt = _expert_major_rows(router_w[i].T)
        x1, h2p, logits_t = _out_projection(y, w_out, xt, modb, g_ffn, router_wt, seq)
        xt = _moe(x1, h2p, logits_t, modb, _expert_major_rows(router_bias[i]),
                  exp_w_gu, exp_w_down, i,
                  shared_w_gu[i].astype(BF16), shared_w_down[i].astype(BF16),
                  final_norm, seq, final=(i == depth - 1))
    return xt.reshape(batch, seq, d)
```

```python
import functools

import jax
import jax.numpy as jnp
from jax import lax
from jax.experimental import pallas as pl
from jax.experimental.pallas import tpu as pltpu

F32 = jnp.float32
BF16 = jnp.bfloat16
I32 = jnp.int32
U32 = jnp.uint32

RET_HEADS = 8
ROPE_BASE = 10000.0
CONV_WIDTH = 31
CONV_HALO = 32
N_GROUPS = 8
TOPK_GROUPS = 4
TOP_K = 8
ROUTED_SCALE = 2.5
NORM_EPS = 1e-6
MOD_ROWS = 8
LANES = 128
SUBLANES = 8
ROW_SUB = 8
DMA_THREADS = 2
EXPERT_BLOCK_ROWS = 512
VMEM_LIMIT = 56 * 1024 * 1024


def _cparams(*sem):
    return pltpu.CompilerParams(dimension_semantics=sem, vmem_limit_bytes=VMEM_LIMIT)


def _tile(n, pref):
    t = min(n, pref)
    assert n % t == 0, (n, pref)
    return t


def _split_bf16(a):
    hi = a.astype(BF16)
    lo = (a - hi.astype(F32)).astype(BF16)
    return hi, lo


def _dot3(a, b, dims):
    ah, al = _split_bf16(a)
    bh, bl = _split_bf16(b)
    dg = functools.partial(lax.dot_general, dimension_numbers=dims, preferred_element_type=F32)
    return dg(ah, bh) + dg(ah, bl) + dg(al, bh)


_NN = (((1,), (0,)), ((), ()))
_NT = (((1,), (1,)), ((), ()))
_TN = (((0,), (0,)), ((), ()))


def _normmod(x, g, shift, scale):
    y = x * lax.rsqrt(jnp.mean(x * x, axis=-1, keepdims=True) + NORM_EPS)
    return (y * g) * (1.0 + scale) + shift


def _silu(x):
    return x * jax.nn.sigmoid(x)


def _pack_rows(h):
    half = h.shape[1] // 2
    bits = lax.bitcast_convert_type(h.astype(BF16).astype(F32), U32)
    lo = lax.shift_right_logical(bits[:, :half], jnp.uint32(16))
    hi = bits[:, half:] & jnp.uint32(0xFFFF0000)
    return hi | lo


def _unpack_rows(w):
    lo = lax.bitcast_convert_type(lax.shift_left(w, jnp.uint32(16)), F32)
    hi = lax.bitcast_convert_type(w & jnp.uint32(0xFFFF0000), F32)
    return lo, hi


def _store_row_tiles(ref, base, packed):
    t = packed.shape[0]
    assert packed.shape[1] == ROW_SUB * LANES
    for s in range(ROW_SUB):
        ref[pl.ds(base + s, t, stride=ROW_SUB), :] = packed[:, s * LANES:(s + 1) * LANES]


def _load_row_tiles(ref, base, t):
    parts = [_unpack_rows(ref[pl.ds(base + s, t, stride=ROW_SUB), :]) for s in range(ROW_SUB)]
    lo = jnp.concatenate([p[0] for p in parts], axis=1)
    hi = jnp.concatenate([p[1] for p in parts], axis=1)
    return lo, hi


def _load_row_tiles_bf16(ref, base, t):
    lo, hi = _load_row_tiles(ref, base, t)
    return jnp.concatenate([lo.astype(BF16), hi.astype(BF16)], axis=1)


def _mod_kernel(c_ref, w_ref, b_ref, o_ref):
    c = c_ref[...]
    o_ref[0] = _dot3(_silu(c), w_ref[0], _NN) + b_ref[0]


def _modulation(c, mod_w, mod_b):
    depth, d, n6 = mod_w.shape
    b = c.shape[0]
    assert b <= MOD_ROWS
    c_pad = jnp.zeros((MOD_ROWS, d), F32).at[:b].set(c)
    tn = _tile(n6, 1024)
    out = pl.pallas_call(
        _mod_kernel,
        grid=(depth, n6 // tn),
        in_specs=[pl.BlockSpec((MOD_ROWS, d), lambda i, j: (0, 0)),
                  pl.BlockSpec((1, d, tn), lambda i, j: (i, 0, j)),
                  pl.BlockSpec((1, 1, tn), lambda i, j: (i, 0, j))],
        out_specs=pl.BlockSpec((1, MOD_ROWS, tn), lambda i, j: (i, 0, j)),
        out_shape=jax.ShapeDtypeStruct((depth, MOD_ROWS, n6), F32),
        compiler_params=_cparams("arbitrary", "arbitrary"),
        name="adaln_modulation",
    )(c_pad, mod_w, mod_b.reshape(depth, 1, n6))
    return out[:, :b]


def _mod_block(mod_i, d):
    b = mod_i.shape[0]
    m = mod_i.reshape(b, 6, d)
    return jnp.concatenate([m, jnp.zeros((b, MOD_ROWS - 6, d), F32)], axis=1)


def _proj_kernel(x_ref, g_ref, mod_ref, w_ref, o_ref, h_scr):
    @pl.when(pl.program_id(1) == 0)
    def _():
        m = mod_ref[0]
        h_scr[...] = _normmod(x_ref[...], g_ref[...], m[0:1], m[1:2]).astype(BF16)

    o_ref[...] = jnp.dot(h_scr[...], w_ref[...], preferred_element_type=F32).astype(o_ref.dtype)


def _glu_proj_kernel(x_ref, g_ref, mod_ref, wa_ref, wb_ref, o_ref, h_scr):
    @pl.when(pl.program_id(1) == 0)
    def _():
        m = mod_ref[0]
        h_scr[...] = _normmod(x_ref[...], g_ref[...], m[0:1], m[1:2]).astype(BF16)

    h = h_scr[...]
    a = jnp.dot(h, wa_ref[...], preferred_element_type=F32)
    b = jnp.dot(h, wb_ref[...], preferred_element_type=F32)
    o_ref[...] = a * jax.nn.sigmoid(b)


def _normmod_proj(x, g, modb, w, seq, glu):
    n, d = x.shape
    nout = w.shape[1] // 2 if glu else w.shape[1]
    tm = _tile(seq, 1024)
    tn = _tile(nout, 512 if glu else 1024)
    tiles_per_seq = seq // tm
    x_spec = pl.BlockSpec((tm, d), lambda i, j: (i, 0))
    g_spec = pl.BlockSpec((1, d), lambda i, j: (0, 0))
    m_spec = pl.BlockSpec((1, MOD_ROWS, d), lambda i, j: (i // tiles_per_seq, 0, 0))
    if glu:
        half_blocks = nout // tn
        in_specs = [x_spec, g_spec, m_spec,
                    pl.BlockSpec((d, tn), lambda i, j: (0, j)),
                    pl.BlockSpec((d, tn), lambda i, j: (0, j + half_blocks))]
        body, args, odt = _glu_proj_kernel, (x, g, modb, w, w), F32
    else:
        in_specs = [x_spec, g_spec, m_spec, pl.BlockSpec((d, tn), lambda i, j: (0, j))]
        body, args, odt = _proj_kernel, (x, g, modb, w), BF16
    return pl.pallas_call(
        body,
        grid=(n // tm, nout // tn),
        in_specs=in_specs,
        out_specs=pl.BlockSpec((tm, tn), lambda i, j: (i, j)),
        out_shape=jax.ShapeDtypeStruct((n, nout), odt),
        scratch_shapes=[pltpu.VMEM((tm, d), BF16)],
        compiler_params=_cparams("arbitrary", "arbitrary"),
        name="glu_in_projection" if glu else "in_projection",
    )(*args)


def _retention_kernel(pos_ref, inv_ref, q_ref, k_ref, v_ref, g_ref, intra_ref, qd_ref, kd_ref,
                      cd_ref, gain_ref, o_ref, state, *, dk, dv):
    @pl.when(pl.program_id(1) == 0)
    def _():
        state[...] = jnp.zeros_like(state)

    half = dk // 2
    ang = pos_ref[...].astype(F32) * inv_ref[...]
    cos = jnp.cos(ang)
    sin = jnp.sin(ang)

    def rot(ref, h):
        x1 = ref[:, h * dk:h * dk + half].astype(F32)
        x2 = ref[:, h * dk + half:(h + 1) * dk].astype(F32)
        return jnp.concatenate([x1 * cos - x2 * sin, x1 * sin + x2 * cos], axis=1)

    for h in range(RET_HEADS):
        q = rot(q_ref, h)
        k = rot(k_ref, h) * (dk ** -0.5)
        v = v_ref[:, h * dv:(h + 1) * dv]
        s = lax.dot_general(q.astype(BF16), k.astype(BF16), _NT, preferred_element_type=F32)
        p = (s * intra_ref[h]).astype(BF16)
        inner = jnp.dot(p, v, preferred_element_type=F32)
        st = state[h]
        cross = jnp.dot((q * qd_ref[h]).astype(BF16), st.astype(BF16),
                        preferred_element_type=F32)
        kv = lax.dot_general((k * kd_ref[h]).astype(BF16), v, _TN, preferred_element_type=F32)
        state[h] = cd_ref[h][:, :1] * st + kv
        o = inner + cross
        o = o * lax.rsqrt(jnp.mean(o * o, axis=-1, keepdims=True) + NORM_EPS)
        gate = g_ref[:, h * dv:(h + 1) * dv].astype(F32)
        o_ref[:, h * dv:(h + 1) * dv] = (_silu(gate) * (o * gain_ref[:, h * dv:(h + 1) * dv])
                                         ).astype(o_ref.dtype)


def _retention_core(proj, positions, out_gain, batch, seq, d):
    n = batch * seq
    heads, dk = RET_HEADS, d // RET_HEADS
    dv = 2 * dk
    c = _tile(seq, 128)
    nc = seq // c
    half = dk // 2
    inv = (1.0 / (ROPE_BASE ** jnp.linspace(0.0, 1.0, half, dtype=F32))).reshape(1, half)
    log_gamma = jnp.log1p(-jnp.exp2(-5.0 - jnp.arange(heads, dtype=F32)))
    idx = jnp.arange(c, dtype=F32)
    rel = idx[:, None] - idx[None, :]
    intra = jnp.where(rel >= 0, jnp.exp(log_gamma[:, None, None] * jnp.maximum(rel, 0.0)), 0.0)
    q_decay = jnp.exp(log_gamma[:, None] * (idx + 1.0))[:, :, None]
    k_decay = jnp.exp(log_gamma[:, None] * (c - 1.0 - idx))[:, :, None]
    chunk_decay = jnp.broadcast_to(jnp.exp(log_gamma * c)[:, None, None], (heads, 1, LANES))
    row = lambda b, j: b * nc + j
    full3 = lambda b, j: (0, 0, 0)
    return pl.pallas_call(
        functools.partial(_retention_kernel, dk=dk, dv=dv),
        grid=(batch, nc),
        in_specs=[pl.BlockSpec((c, 1), lambda b, j: (row(b, j), 0)),
                  pl.BlockSpec((1, half), lambda b, j: (0, 0)),
                  pl.BlockSpec((c, heads * dk), lambda b, j: (row(b, j), 0)),
                  pl.BlockSpec((c, heads * dk), lambda b, j: (row(b, j), 1)),
                  pl.BlockSpec((c, heads * dv), lambda b, j: (row(b, j), 1)),
                  pl.BlockSpec((c, heads * dv), lambda b, j: (row(b, j), 2)),
                  pl.BlockSpec((heads, c, c), full3),
                  pl.BlockSpec((heads, c, 1), full3),
                  pl.BlockSpec((heads, c, 1), full3),
                  pl.BlockSpec((heads, 1, LANES), full3),
                  pl.BlockSpec((1, heads * dv), lambda b, j: (0, 0))],
        out_specs=pl.BlockSpec((c, heads * dv), lambda b, j: (row(b, j), 0)),
        out_shape=jax.ShapeDtypeStruct((n, heads * dv), BF16),
        scratch_shapes=[pltpu.VMEM((heads, dk, dv), F32)],
        compiler_params=_cparams("arbitrary", "arbitrary"),
        name="retention_core",
    )(positions.reshape(n, 1), inv, proj, proj, proj, proj, intra, q_decay, k_decay,
      chunk_decay, out_gain.reshape(1, heads * dv))


def _conv_kernel(u_ref, halo_ref, w_ref, b_ref, lg_ref, lb_ref, o_ref, ext, conv, *, tile):
    first = pl.program_id(1) == 0
    ext[0, 0:CONV_HALO, :] = jnp.where(first, 0.0, halo_ref[...])
    ext[0, CONV_HALO:CONV_HALO + tile, :] = u_ref[...]
    span = ext.shape[1] - SUBLANES
    for s in range(1, SUBLANES):
        ext[s, 0:span, :] = ext[0, s:s + span, :]
    base = CONV_HALO - (CONV_WIDTH - 1)
    ch = u_ref.shape[1]
    cw = min(ch, 4 * LANES)
    for c0 in range(0, ch, cw):
        part = jnp.zeros((tile, cw), F32) + b_ref[:, c0:c0 + cw]
        for j in range(CONV_WIDTH):
            shift, start = (base + j) % SUBLANES, (base + j) // SUBLANES * SUBLANES
            part = part + ext[shift, start:start + tile, c0:c0 + cw] * w_ref[j:j + 1, c0:c0 + cw]
        conv[:, c0:c0 + cw] = part
    acc = conv[...]
    mu = jnp.mean(acc, axis=-1, keepdims=True)
    cen = acc - mu
    var = jnp.mean(cen * cen, axis=-1, keepdims=True)
    y = cen * lax.rsqrt(var + NORM_EPS) * lg_ref[...] + lb_ref[...]
    o_ref[...] = _silu(y).astype(o_ref.dtype)


def _conv_ln_silu(u, dw_w, dw_b, ln_g, ln_b, batch, seq):
    n, ch = u.shape
    t = _tile(seq, 64)
    assert t % CONV_HALO == 0
    nt = seq // t
    halo_per_tile = t // CONV_HALO
    row = lambda b, j: b * nt + j
    vec = pl.BlockSpec((1, ch), lambda b, j: (0, 0))
    return pl.pallas_call(
        functools.partial(_conv_kernel, tile=t),
        grid=(batch, nt),
        in_specs=[pl.BlockSpec((t, ch), lambda b, j: (row(b, j), 0)),
                  pl.BlockSpec((CONV_HALO, ch),
                               lambda b, j: (jnp.maximum(row(b, j) * halo_per_tile - 1, 0), 0)),
                  pl.BlockSpec((CONV_WIDTH, ch), lambda b, j: (0, 0)),
                  vec, vec, vec],
        out_specs=pl.BlockSpec((t, ch), lambda b, j: (row(b, j), 0)),
        out_shape=jax.ShapeDtypeStruct((n, ch), BF16),
        scratch_shapes=[pltpu.VMEM((SUBLANES, CONV_HALO + t, ch), F32),
                        pltpu.VMEM((t, ch), F32)],
        compiler_params=_cparams("arbitrary", "arbitrary"),
        name="conv_ln_silu",
    )(u, u, dw_w, dw_b.reshape(1, ch), ln_g.reshape(1, ch), ln_b.reshape(1, ch))


def _outproj_kernel(y_ref, w_ref, x_ref, mod_ref, g_ref, rw_ref, x1_ref, h2_ref, lg_ref, acc):
    k = pl.program_id(1)

    @pl.when(k == 0)
    def _():
        acc[...] = jnp.zeros_like(acc)

    acc[...] += jnp.dot(y_ref[...], w_ref[...], preferred_element_type=F32)

    @pl.when(k == pl.num_programs(1) - 1)
    def _():
        m = mod_ref[0]
        x1 = x_ref[...] + m[2:3] * acc[...]
        x1_ref[...] = x1
        h2 = _normmod(x1, g_ref[...], m[3:4], m[4:5])
        _store_row_tiles(h2_ref, 0, _pack_rows(h2))
        lg_ref[...] = _dot3(rw_ref[...], h2, _NT)


def _out_projection(y, w, x, modb, g_ffn, router_wt, seq):
    n, kdim = y.shape
    d = x.shape[1]
    ne = router_wt.shape[0]
    assert d // 2 == ROW_SUB * LANES
    tm = _tile(seq, 512)
    tk = _tile(kdim, 1024)
    tiles_per_seq = seq // tm
    return pl.pallas_call(
        _outproj_kernel,
        grid=(n // tm, kdim // tk),
        in_specs=[pl.BlockSpec((tm, tk), lambda i, k: (i, k)),
                  pl.BlockSpec((tk, d), lambda i, k: (k, 0)),
                  pl.BlockSpec((tm, d), lambda i, k: (i, 0)),
                  pl.BlockSpec((1, MOD_ROWS, d), lambda i, k: (i // tiles_per_seq, 0, 0)),
                  pl.BlockSpec((1, d), lambda i, k: (0, 0)),
                  pl.BlockSpec((ne, d), lambda i, k: (0, 0))],
        out_specs=[pl.BlockSpec((tm, d), lambda i, k: (i, 0)),
                   pl.BlockSpec((tm * ROW_SUB, LANES), lambda i, k: (i, 0)),
                   pl.BlockSpec((ne, tm), lambda i, k: (0, i))],
        out_shape=[jax.ShapeDtypeStruct((n, d), F32),
                   jax.ShapeDtypeStruct((n * ROW_SUB, LANES), U32),
                   jax.ShapeDtypeStruct((ne, n), F32)],
        scratch_shapes=[pltpu.VMEM((tm, d), F32)],
        compiler_params=_cparams("arbitrary", "arbitrary"),
        name="out_projection",
    )(y, w, x, modb, g_ffn, router_wt)


def _routing_kernel(lg_ref, bias_ref, eidx_ref, wts_ref):
    neg = -jnp.inf
    per = N_GROUPS
    tr = lg_ref.shape[1]
    scores = jax.nn.sigmoid(lg_ref[...])
    biased = scores + bias_ref[...]
    n_slab = lg_ref.shape[0] // per
    s_j = [scores[per * j:per * (j + 1), :] for j in range(n_slab)]
    b_j = [biased[per * j:per * (j + 1), :] for j in range(n_slab)]

    m1 = functools.reduce(jnp.maximum, b_j)
    ties = functools.reduce(lambda a, b: a + b, [(b == m1).astype(F32) for b in b_j])
    below = functools.reduce(jnp.maximum, [jnp.where(b < m1, b, neg) for b in b_j])
    grp = m1 + jnp.where(ties >= 2.0, m1, below)

    gid = lax.broadcasted_iota(I32, (per, tr), 0).astype(F32)
    chosen = jnp.zeros((per, tr), F32)
    cur = grp
    for _ in range(TOPK_GROUPS):
        mx = jnp.max(cur, axis=0, keepdims=True)
        first = jnp.min(jnp.where(cur == mx, gid, float(per)), axis=0, keepdims=True)
        pick = gid == first
        chosen = jnp.where(pick, 1.0, chosen)
        cur = jnp.where(pick, neg, cur)

    eid_j = [gid * float(n_slab) + float(j) for j in range(n_slab)]
    cur_j = [jnp.where(chosen > 0.0, b, neg) for b in b_j]
    picked_scores = []
    for k in range(TOP_K):
        mx = jnp.max(functools.reduce(jnp.maximum, cur_j), axis=0, keepdims=True)
        cand = functools.reduce(
            jnp.minimum, [jnp.where(c == mx, e, float(per * n_slab)) for c, e in zip(cur_j, eid_j)])
        first = jnp.min(cand, axis=0, keepdims=True)
        pick_j = [e == first for e in eid_j]
        sc = functools.reduce(lambda a, b: a + b,
                              [jnp.where(p, s, 0.0) for p, s in zip(pick_j, s_j)])
        picked_scores.append(jnp.sum(sc, axis=0, keepdims=True))
        cur_j = [jnp.where(p, neg, c) for p, c in zip(pick_j, cur_j)]
        eidx_ref[k:k + 1, :] = first.astype(I32)
    total = functools.reduce(lambda a, b: a + b, picked_scores)
    for k in range(TOP_K):
        wts_ref[k:k + 1, :] = picked_scores[k] / total * ROUTED_SCALE


def _routing(logits_t, bias_perm):
    ne, n = logits_t.shape
    tr = _tile(n, 1024)
    return pl.pallas_call(
        _routing_kernel,
        grid=(n // tr,),
        in_specs=[pl.BlockSpec((ne, tr), lambda i: (0, i)),
                  pl.BlockSpec((ne, 1), lambda i: (0, 0))],
        out_specs=[pl.BlockSpec((TOP_K, tr), lambda i: (0, i)),
                   pl.BlockSpec((TOP_K, tr), lambda i: (0, i))],
        out_shape=[jax.ShapeDtypeStruct((TOP_K, n), I32),
                   jax.ShapeDtypeStruct((TOP_K, n), F32)],
        compiler_params=_cparams("arbitrary"),
        name="moe_routing",
    )(logits_t, bias_perm.reshape(ne, 1))


def _plan_kernel(e_ref, u_ref, l_ref, dest_ref, be_ref, nu_ref, ss_ref, sn_ref, cnt, base, carry,
                 *, tb, ne):
    p = pl.program_id(0)
    t = pl.program_id(1)
    tp = e_ref.shape[1]
    ei = e_ref[...]
    eid = lax.broadcasted_iota(I32, (ne, tp), 0)
    hit = [ei[k:k + 1, :] == eid for k in range(TOP_K)]
    onehot = functools.reduce(lambda a, b: a + b, [h.astype(F32) for h in hit])
    tile_cnt = jnp.sum(onehot, axis=1, keepdims=True)

    @pl.when((p == 0) & (t == 0))
    def _():
        cnt[...] = jnp.zeros_like(cnt)

    @pl.when(p == 0)
    def _():
        cnt[...] += tile_cnt

    @pl.when((p == 1) & (t == 0))
    def _():
        nblk = jnp.floor((cnt[...] + float(tb - 1)) * (1.0 / tb))
        start_blk = jnp.dot(l_ref[...], nblk.astype(BF16), preferred_element_type=F32)
        base[...] = start_blk * float(tb)
        carry[...] = jnp.zeros_like(carry)
        end_blk = start_blk + nblk
        nbp = be_ref.shape[1]
        blk = lax.broadcasted_iota(I32, (ne, nbp), 1).astype(F32)
        owner = jnp.sum((end_blk[:, :1] <= blk).astype(F32), axis=0, keepdims=True)
        be_ref[...] = jnp.minimum(owner, float(ne - 1)).astype(I32)
        nu_ref[...] = end_blk[ne - 1:ne, :].astype(I32)
        diag = (lax.broadcasted_iota(I32, (ne, LANES), 0)
                == lax.broadcasted_iota(I32, (ne, LANES), 1))
        ss_ref[...] = jnp.sum(jnp.where(diag, start_blk, 0.0), axis=0, keepdims=True).astype(I32)
        sn_ref[...] = jnp.sum(jnp.where(diag, nblk, 0.0), axis=0, keepdims=True).astype(I32)

    @pl.when(p == 1)
    def _():
        before = jnp.dot(onehot.astype(BF16), u_ref[...], preferred_element_type=F32)
        rowpos = base[:, :1] + carry[:, :1] + before
        for k in range(TOP_K):
            dest_ref[k:k + 1, :] = jnp.sum(jnp.where(hit[k], rowpos, 0.0), axis=0,
                                           keepdims=True).astype(I32)
        carry[...] += tile_cnt


def _dispatch_plan(eidx, ne, tb, n_blocks):
    n = eidx.shape[1]
    tp = _tile(n, 512)
    assert n // tb + 1 <= 256
    assert ne <= LANES
    lane_row = pl.BlockSpec((1, LANES), lambda p, t: (0, 0))
    nbp = -(-n_blocks // LANES) * LANES
    upper = (jnp.arange(tp)[:, None] < jnp.arange(tp)[None, :]).astype(BF16)
    lower = (jnp.arange(ne)[None, :] < jnp.arange(ne)[:, None]).astype(BF16)
    return pl.pallas_call(
        functools.partial(_plan_kernel, tb=tb, ne=ne),
        grid=(2, n // tp),
        in_specs=[pl.BlockSpec((TOP_K, tp), lambda p, t: (0, t)),
                  pl.BlockSpec((tp, tp), lambda p, t: (0, 0)),
                  pl.BlockSpec((ne, ne), lambda p, t: (0, 0))],
        out_specs=[pl.BlockSpec((TOP_K, tp), lambda p, t: (0, t * p)),
                   pl.BlockSpec((1, nbp), lambda p, t: (0, 0)),
                   lane_row, lane_row, lane_row],
        out_shape=[jax.ShapeDtypeStruct((TOP_K, n), I32),
                   jax.ShapeDtypeStruct((1, nbp), I32)]
        + [jax.ShapeDtypeStruct((1, LANES), I32)] * 3,
        scratch_shapes=[pltpu.VMEM((ne, LANES), F32)] * 3,
        compiler_params=_cparams("arbitrary", "arbitrary"),
        name="moe_dispatch_plan",
    )(eidx, upper, lower)


def _rows(ref, row, count=1):
    return ref.at[pl.ds(pl.multiple_of(row * ROW_SUB, ROW_SUB), count * ROW_SUB), :]


def _row_copy(src, s_row, dst, d_row, sem):
    return pltpu.make_async_copy(_rows(src, s_row), _rows(dst, d_row), sem)


def _dispatch_kernel(ss_ref, sn_ref, dest_ref, h_ref, xs_hbm, zeros, sem, zsem, *, td, tb, ne):
    @pl.when(pl.program_id(0) == 0)
    def _():
        zeros[...] = jnp.zeros_like(zeros)

        def fill_copy(blk):
            return pltpu.make_async_copy(zeros, _rows(xs_hbm, blk * tb, tb), zsem)

        def fill(e, carry):
            @pl.when(sn_ref[e] > 0)
            def _():
                fill_copy(ss_ref[e] + sn_ref[e] - 1).start()
            return carry

        def filled(e, carry):
            @pl.when(sn_ref[e] > 0)
            def _():
                fill_copy(0).wait()
            return carry

        lax.fori_loop(0, ne, fill, 0)
        lax.fori_loop(0, ne, filled, 0)

    def issue(t, carry):
        for k in range(TOP_K):
            _row_copy(h_ref, t, xs_hbm, dest_ref[0, k, t], sem).start(priority=k % DMA_THREADS)
        return carry

    lax.fori_loop(0, td, issue, 0)

    def drain(t, carry):
        for _ in range(TOP_K):
            _row_copy(h_ref, 0, xs_hbm, 0, sem).wait()
        return carry

    lax.fori_loop(0, td, drain, 0)


def _dispatch(dest3, h2p, seg_start, seg_nblk, n_rows, tb, ne):
    nt, _, td = dest3.shape
    grid_spec = pltpu.PrefetchScalarGridSpec(
        num_scalar_prefetch=2,
        grid=(nt,),
        in_specs=[pl.BlockSpec((1, TOP_K, td), lambda i, ss, sn: (i, 0, 0),
                               memory_space=pltpu.SMEM),
                  pl.BlockSpec((td * ROW_SUB, LANES), lambda i, ss, sn: (i, 0))],
        out_specs=pl.BlockSpec(memory_space=pl.ANY),
        scratch_shapes=[pltpu.VMEM((tb * ROW_SUB, LANES), U32),
                        pltpu.SemaphoreType.DMA(()), pltpu.SemaphoreType.DMA(())])
    return pl.pallas_call(
        functools.partial(_dispatch_kernel, td=td, tb=tb, ne=ne),
        grid_spec=grid_spec,
        out_shape=jax.ShapeDtypeStruct((n_rows * ROW_SUB, LANES), U32),
        compiler_params=_cparams("arbitrary"),
        name="moe_row_dispatch",
    )(seg_start, seg_nblk, dest3, h2p)


def _expert_kernel(be_ref, nu_ref, sn_ref, x_ref, wgu_hbm, wdn_hbm, o_ref, gu_f32, dn_f32, wgu_b,
                   wdn_b, sem, *, layer, ne):
    i = pl.program_id(0)
    live = i < nu_ref[0]
    e = be_ref[i]
    new_expert = (i == 0) | (e != be_ref[jnp.maximum(i - 1, 0)])

    def weight_copies(expert):
        return (pltpu.make_async_copy(wgu_hbm.at[layer, expert], gu_f32, sem.at[0]),
                pltpu.make_async_copy(wdn_hbm.at[layer, expert], dn_f32, sem.at[1]))

    @pl.when(live & (i == 0))
    def _():
        for cp in weight_copies(e):
            cp.start()

    @pl.when(live & new_expert)
    def _():
        for cp in weight_copies(e):
            cp.wait()
        wgu_b[...] = gu_f32[...].astype(BF16)
        wdn_b[...] = dn_f32[...].astype(BF16)
        nxt = lax.while_loop(lambda j: (j < ne) & (sn_ref[jnp.minimum(j, ne - 1)] == 0),
                             lambda j: j + 1, e + 1)

        @pl.when(nxt < ne)
        def _():
            for cp in weight_copies(nxt):
                cp.start()

    @pl.when(live)
    def _():
        ff = wdn_b.shape[0]
        tb = x_ref.shape[0] // ROW_SUB
        x = _load_row_tiles_bf16(x_ref, 0, tb)
        gu = jnp.dot(x, wgu_b[...], preferred_element_type=F32)
        act = (_silu(gu[:, :ff]) * gu[:, ff:]).astype(BF16)
        _store_row_tiles(o_ref, 0, _pack_rows(jnp.dot(act, wdn_b[...], preferred_element_type=F32)))


def _expert_ffn(xs, block_e, n_used, seg_nblk, w_gu, w_dn, layer, tb):
    _, ne, d, ff2 = w_gu.shape
    ff = w_dn.shape[2]
    nb = xs.shape[0] // (tb * ROW_SUB)
    row_block = pl.BlockSpec((tb * ROW_SUB, LANES),
                             lambda i, be, nu, sn: (jnp.minimum(i, nu[0] - 1), 0))
    grid_spec = pltpu.PrefetchScalarGridSpec(
        num_scalar_prefetch=3,
        grid=(nb,),
        in_specs=[row_block, pl.BlockSpec(memory_space=pl.ANY), pl.BlockSpec(memory_space=pl.ANY)],
        out_specs=row_block,
        scratch_shapes=[pltpu.VMEM((d, ff2), F32), pltpu.VMEM((ff, d), F32),
                        pltpu.VMEM((d, ff2), BF16), pltpu.VMEM((ff, d), BF16),
                        pltpu.SemaphoreType.DMA((2,))])
    return pl.pallas_call(
        functools.partial(_expert_kernel, layer=layer, ne=ne),
        grid_spec=grid_spec,
        out_shape=jax.ShapeDtypeStruct(xs.shape, U32),
        compiler_params=_cparams("arbitrary"),
        name="moe_expert_ffn",
    )(block_e, n_used, seg_nblk, xs, w_gu, w_dn)


def _combine_kernel(dest_ref, next_ref, wts_ref, ys_hbm, h_ref, x_ref, mod_ref, sgu_ref, sdn_ref,
                    fn_ref, o_ref, gbuf, sem, *, tc, final):
    i = pl.program_id(0)
    slot_rows = TOP_K * tc

    def gather(d_ref, slot):
        def issue(t, carry):
            for k in range(TOP_K):
                _row_copy(ys_hbm, d_ref[0, k, t], gbuf, slot * slot_rows + k * tc + t,
                          sem.at[slot]).start(priority=k % DMA_THREADS)
            return carry

        lax.fori_loop(0, tc, issue, 0)

    @pl.when(i == 0)
    def _():
        gather(dest_ref, 0)

    @pl.when(i + 1 < pl.num_programs(0))
    def _():
        gather(next_ref, (i + 1) % 2)

    ff = sdn_ref.shape[0]
    gu = jnp.dot(_load_row_tiles_bf16(h_ref, 0, tc), sgu_ref[...], preferred_element_type=F32)
    act = (_silu(gu[:, :ff]) * gu[:, ff:]).astype(BF16)
    shared = jnp.dot(act, sdn_ref[...], preferred_element_type=F32)

    slot = i % 2

    def drain(t, carry):
        for _ in range(TOP_K):
            _row_copy(ys_hbm, 0, gbuf, 0, sem.at[slot]).wait()
        return carry

    lax.fori_loop(0, tc, drain, 0)

    w = wts_ref[...]
    base = slot * slot_rows * ROW_SUB
    lo, hi = _load_row_tiles(gbuf, base, tc)
    acc_lo, acc_hi = w[:, 0:1] * lo, w[:, 0:1] * hi
    for k in range(1, TOP_K):
        lo, hi = _load_row_tiles(gbuf, base + k * tc * ROW_SUB, tc)
        acc_lo = acc_lo + w[:, k:k + 1] * lo
        acc_hi = acc_hi + w[:, k:k + 1] * hi
    moe = jnp.concatenate([acc_lo, acc_hi], axis=1) + shared
    x2 = x_ref[...] + mod_ref[0][5:6] * moe
    if final:
        x2 = x2 * lax.rsqrt(jnp.mean(x2 * x2, axis=-1, keepdims=True) + NORM_EPS) * fn_ref[...]
    o_ref[...] = x2


def _combine(dest3, wts_t, ys, h2p, x1, modb, s_gu, s_dn, final_norm, seq, final):
    n, d = x1.shape
    nt, _, tc = dest3.shape
    tiles_per_seq = seq // tc
    full2 = lambda i: (0, 0)
    return pl.pallas_call(
        functools.partial(_combine_kernel, tc=tc, final=final),
        grid=(nt,),
        in_specs=[pl.BlockSpec((1, TOP_K, tc), lambda i: (i, 0, 0), memory_space=pltpu.SMEM),
                  pl.BlockSpec((1, TOP_K, tc), lambda i: (jnp.minimum(i + 1, nt - 1), 0, 0),
                               memory_space=pltpu.SMEM),
                  pl.BlockSpec((tc, TOP_K), lambda i: (i, 0)),
                  pl.BlockSpec(memory_space=pl.ANY),
                  pl.BlockSpec((tc * ROW_SUB, LANES), lambda i: (i, 0)),
                  pl.BlockSpec((tc, d), lambda i: (i, 0)),
                  pl.BlockSpec((1, MOD_ROWS, d), lambda i: (i // tiles_per_seq, 0, 0)),
                  pl.BlockSpec(s_gu.shape, full2),
                  pl.BlockSpec(s_dn.shape, full2),
                  pl.BlockSpec((1, d), full2)],
        out_specs=pl.BlockSpec((tc, d), lambda i: (i, 0)),
        out_shape=jax.ShapeDtypeStruct((n, d), F32),
        scratch_shapes=[pltpu.VMEM((2 * TOP_K * tc * ROW_SUB, LANES), U32),
                        pltpu.SemaphoreType.DMA((2,))],
        compiler_params=_cparams("arbitrary"),
        name="moe_combine",
    )(dest3, dest3, wts_t, ys, h2p, x1, modb, s_gu, s_dn, final_norm.reshape(1, d))


def _moe(x1, h2p, logits_t, modb, bias_perm, w_gu, w_dn, layer, s_gu, s_dn, final_norm, seq,
         final):
    n, d = x1.shape
    ne = w_gu.shape[1]
    tb = EXPERT_BLOCK_ROWS
    tok_tile = _tile(seq, 256)
    n_blocks = n * TOP_K // tb + ne
    eidx, wts = _routing(logits_t, bias_perm)
    dest, block_e, n_used, seg_start, seg_nblk = _dispatch_plan(eidx, ne, tb, n_blocks)
    dest3 = dest.reshape(TOP_K, n // tok_tile, tok_tile).transpose(1, 0, 2)
    xs = _dispatch(dest3, h2p, seg_start[0, :ne], seg_nblk[0, :ne], n_blocks * tb, tb, ne)
    ys = _expert_ffn(xs, block_e[0, :n_blocks], n_used[0, :1], seg_nblk[0, :ne], w_gu, w_dn,
                     layer, tb)
    return _combine(dest3, wts.T, ys, h2p, x1, modb, s_gu, s_dn, final_norm, seq, final)


def _expert_major_rows(a):
    per = a.shape[0] // N_GROUPS
    return a.reshape((N_GROUPS, per) + a.shape[1:]).swapaxes(0, 1).reshape(a.shape)


def _pairs_to_halves(w, heads):
    d, cols = w.shape
    dk = cols // heads
    return w.reshape(d, heads, dk // 2, 2).transpose(0, 1, 3, 2).reshape(d, cols)


def kernel(x, c, positions, mod_w, mod_b, norm_mix, norm_ffn, ret_w_in, ret_w_out, ret_out_gain, conv_w_in, conv_dw_w, conv_dw_b, conv_ln_g, conv_ln_b, conv_w_out, router_w, router_bias, exp_w_gu, exp_w_down, shared_w_gu, shared_w_down, final_norm):
    batch, seq, d = x.shape
    n = batch * seq
    depth = mod_w.shape[0]
    heads = RET_HEADS
    qk_cols = d
    mods = _modulation(c, mod_w, mod_b)
    xt = x.reshape(n, d)
    for i in range(depth):
        modb = _mod_block(mods[i], d)
        g_mix = norm_mix[i].reshape(1, d)
        g_ffn = norm_ffn[i].reshape(1, d)
        j = i // 2
        if i % 2 == 0:
            w_in = ret_w_in[j]
            w_in = jnp.concatenate([_pairs_to_halves(w_in[:, :qk_cols], heads),
                                    _pairs_to_halves(w_in[:, qk_cols:2 * qk_cols], heads),
                                    w_in[:, 2 * qk_cols:]], axis=1).astype(BF16)
            proj = _normmod_proj(xt, g_mix, modb, w_in, seq, glu=False)
            y = _retention_core(proj, positions, ret_out_gain[j], batch, seq, d)
            w_out = ret_w_out[j].astype(BF16)
        else:
            u = _normmod_proj(xt, g_mix, modb, conv_w_in[j].astype(BF16), seq, glu=True)
            y = _conv_ln_silu(u, conv_dw_w[j], conv_dw_b[j], conv_ln_g[j], conv_ln_b[j], batch, seq)
            w_out = conv_w_out[j].astype(BF16)
        router_wt = _expert_major_rows(router_w[i].T)
        x1, h2p, logits_t = _out_projection(y, w_out, xt, modb, g_ffn, router_wt, seq)
        xt = _moe(x1, h2p, logits_t, modb, _expert_major_rows(router_bias[i]),
                  exp_w_gu, exp_w_down, i,
                  shared_w_gu[i].astype(BF16), shared_w_down[i].astype(BF16),
                  final_norm, seq, final=(i == depth - 1))
    return xt.reshape(batch, seq, d)
```

```python
import functools

import jax
import jax.numpy as jnp
from jax import lax
from jax.experimental import pallas as pl
from jax.experimental.pallas import tpu as pltpu

F32 = jnp.float32
BF16 = jnp.bfloat16
I32 = jnp.int32
U32 = jnp.uint32

RET_HEADS = 8
ROPE_BASE = 10000.0
CONV_WIDTH = 31
CONV_HALO = 32
CONV_ACC_VREGS = 32
N_GROUPS = 8
TOPK_GROUPS = 4
TOP_K = 8
ROUTED_SCALE = 2.5
NORM_EPS = 1e-6
MOD_ROWS = 8
LANES = 128
SUBLANES = 8
ROW_SUB = 8
COMBINE_GROUP = 16
DMA_THREADS = 2
EXPERT_BLOCK_ROWS = 512
VMEM_LIMIT = 56 * 1024 * 1024


def _cparams(*sem):
    return pltpu.CompilerParams(dimension_semantics=sem, vmem_limit_bytes=VMEM_LIMIT)


def _tile(n, pref):
    t = min(n, pref)
    assert n % t == 0, (n, pref)
    return t


def _split_bf16(a):
    hi = a.astype(BF16)
    lo = (a - hi.astype(F32)).astype(BF16)
    return hi, lo


def _dot3(a, b, dims):
    ah, al = _split_bf16(a)
    bh, bl = _split_bf16(b)
    dg = functools.partial(lax.dot_general, dimension_numbers=dims, preferred_element_type=F32)
    return dg(ah, bh) + dg(ah, bl) + dg(al, bh)


_NN = (((1,), (0,)), ((), ()))
_NT = (((1,), (1,)), ((), ()))
_TN = (((0,), (0,)), ((), ()))


def _normmod(x, g, shift, scale):
    y = x * lax.rsqrt(jnp.mean(x * x, axis=-1, keepdims=True) + NORM_EPS)
    return (y * g) * (1.0 + scale) + shift


def _silu(x):
    return x * jax.nn.sigmoid(x)


def _pack_rows(h):
    half = h.shape[1] // 2
    bits = lax.bitcast_convert_type(h.astype(BF16).astype(F32), U32)
    lo = lax.shift_right_logical(bits[:, :half], jnp.uint32(16))
    hi = bits[:, half:] & jnp.uint32(0xFFFF0000)
    return hi | lo


def _unpack_rows(w):
    lo = lax.bitcast_convert_type(lax.shift_left(w, jnp.uint32(16)), F32)
    hi = lax.bitcast_convert_type(w & jnp.uint32(0xFFFF0000), F32)
    return lo, hi


def _store_row_tiles(ref, base, packed):
    t = packed.shape[0]
    assert packed.shape[1] == ROW_SUB * LANES
    for s in range(ROW_SUB):
        ref[pl.ds(base + s, t, stride=ROW_SUB), :] = packed[:, s * LANES:(s + 1) * LANES]


def _load_row_tiles(ref, base, t):
    parts = [_unpack_rows(ref[pl.ds(base + s, t, stride=ROW_SUB), :]) for s in range(ROW_SUB)]
    lo = jnp.concatenate([p[0] for p in parts], axis=1)
    hi = jnp.concatenate([p[1] for p in parts], axis=1)
    return lo, hi


def _load_row_tiles_bf16(ref, base, t):
    lo, hi = _load_row_tiles(ref, base, t)
    return jnp.concatenate([lo.astype(BF16), hi.astype(BF16)], axis=1)


def _mod_kernel(c_ref, w_ref, b_ref, o_ref):
    c = c_ref[...]
    o_ref[0] = _dot3(_silu(c), w_ref[0], _NN) + b_ref[0]


def _modulation(c, mod_w, mod_b):
    depth, d, n6 = mod_w.shape
    b = c.shape[0]
    assert b <= MOD_ROWS
    c_pad = jnp.zeros((MOD_ROWS, d), F32).at[:b].set(c)
    tn = _tile(n6, 1024)
    out = pl.pallas_call(
        _mod_kernel,
        grid=(depth, n6 // tn),
        in_specs=[pl.BlockSpec((MOD_ROWS, d), lambda i, j: (0, 0)),
                  pl.BlockSpec((1, d, tn), lambda i, j: (i, 0, j)),
                  pl.BlockSpec((1, 1, tn), lambda i, j: (i, 0, j))],
        out_specs=pl.BlockSpec((1, MOD_ROWS, tn), lambda i, j: (i, 0, j)),
        out_shape=jax.ShapeDtypeStruct((depth, MOD_ROWS, n6), F32),
        compiler_params=_cparams("arbitrary", "arbitrary"),
        name="adaln_modulation",
    )(c_pad, mod_w, mod_b.reshape(depth, 1, n6))
    return out[:, :b]


def _mod_block(mod_i, d):
    b = mod_i.shape[0]
    m = mod_i.reshape(b, 6, d)
    return jnp.concatenate([m, jnp.zeros((b, MOD_ROWS - 6, d), F32)], axis=1)


def _proj_kernel(x_ref, g_ref, mod_ref, w_ref, o_ref, h_scr):
    @pl.when(pl.program_id(1) == 0)
    def _():
        m = mod_ref[0]
        h_scr[...] = _normmod(x_ref[...], g_ref[...], m[0:1], m[1:2]).astype(BF16)

    o_ref[...] = jnp.dot(h_scr[...], w_ref[...], preferred_element_type=F32).astype(o_ref.dtype)


def _glu_proj_kernel(x_ref, g_ref, mod_ref, wa_ref, wb_ref, o_ref, h_scr):
    @pl.when(pl.program_id(1) == 0)
    def _():
        m = mod_ref[0]
        h_scr[...] = _normmod(x_ref[...], g_ref[...], m[0:1], m[1:2]).astype(BF16)

    h = h_scr[...]
    a = jnp.dot(h, wa_ref[...], preferred_element_type=F32)
    b = jnp.dot(h, wb_ref[...], preferred_element_type=F32)
    o_ref[...] = a * jax.nn.sigmoid(b)


def _normmod_proj(x, g, modb, w, seq, glu):
    n, d = x.shape
    nout = w.shape[1] // 2 if glu else w.shape[1]
    tm = _tile(seq, 1024)
    tn = _tile(nout, 512 if glu else 1024)
    tiles_per_seq = seq // tm
    x_spec = pl.BlockSpec((tm, d), lambda i, j: (i, 0))
    g_spec = pl.BlockSpec((1, d), lambda i, j: (0, 0))
    m_spec = pl.BlockSpec((1, MOD_ROWS, d), lambda i, j: (i // tiles_per_seq, 0, 0))
    if glu:
        half_blocks = nout // tn
        in_specs = [x_spec, g_spec, m_spec,
                    pl.BlockSpec((d, tn), lambda i, j: (0, j)),
                    pl.BlockSpec((d, tn), lambda i, j: (0, j + half_blocks))]
        body, args, odt = _glu_proj_kernel, (x, g, modb, w, w), F32
    else:
        in_specs = [x_spec, g_spec, m_spec, pl.BlockSpec((d, tn), lambda i, j: (0, j))]
        body, args, odt = _proj_kernel, (x, g, modb, w), BF16
    return pl.pallas_call(
        body,
        grid=(n // tm, nout // tn),
        in_specs=in_specs,
        out_specs=pl.BlockSpec((tm, tn), lambda i, j: (i, j)),
        out_shape=jax.ShapeDtypeStruct((n, nout), odt),
        scratch_shapes=[pltpu.VMEM((tm, d), BF16)],
        compiler_params=_cparams("arbitrary", "arbitrary"),
        name="glu_in_projection" if glu else "in_projection",
    )(*args)


def _retention_kernel(pos_ref, inv_ref, q_ref, k_ref, v_ref, g_ref, intra_ref, qd_ref, kd_ref,
                      cd_ref, gain_ref, o_ref, state, *, dk, dv):
    @pl.when(pl.program_id(1) == 0)
    def _():
        state[...] = jnp.zeros_like(state)

    half = dk // 2
    ang = pos_ref[...].astype(F32) * inv_ref[...]
    cos = jnp.cos(ang)
    sin = jnp.sin(ang)

    def rot(ref, h):
        x1 = ref[:, h * dk:h * dk + half].astype(F32)
        x2 = ref[:, h * dk + half:(h + 1) * dk].astype(F32)
        return jnp.concatenate([x1 * cos - x2 * sin, x1 * sin + x2 * cos], axis=1)

    for h in range(RET_HEADS):
        q = rot(q_ref, h)
        k = rot(k_ref, h) * (dk ** -0.5)
        v = v_ref[:, h * dv:(h + 1) * dv]
        s = lax.dot_general(q.astype(BF16), k.astype(BF16), _NT, preferred_element_type=F32)
        p = (s * intra_ref[h]).astype(BF16)
        inner = jnp.dot(p, v, preferred_element_type=F32)
        st = state[h]
        cross = jnp.dot((q * qd_ref[h]).astype(BF16), st.astype(BF16),
                        preferred_element_type=F32)
        kv = lax.dot_general((k * kd_ref[h]).astype(BF16), v, _TN, preferred_element_type=F32)
        state[h] = cd_ref[h][:, :1] * st + kv
        o = inner + cross
        o = o * lax.rsqrt(jnp.mean(o * o, axis=-1, keepdims=True) + NORM_EPS)
        gate = g_ref[:, h * dv:(h + 1) * dv].astype(F32)
        o_ref[:, h * dv:(h + 1) * dv] = (_silu(gate) * (o * gain_ref[:, h * dv:(h + 1) * dv])
                                         ).astype(o_ref.dtype)


def _retention_core(proj, positions, out_gain, batch, seq, d):
    n = batch * seq
    heads, dk = RET_HEADS, d // RET_HEADS
    dv = 2 * dk
    c = _tile(seq, 128)
    nc = seq // c
    half = dk // 2
    inv = (1.0 / (ROPE_BASE ** jnp.linspace(0.0, 1.0, half, dtype=F32))).reshape(1, half)
    log_gamma = jnp.log1p(-jnp.exp2(-5.0 - jnp.arange(heads, dtype=F32)))
    idx = jnp.arange(c, dtype=F32)
    rel = idx[:, None] - idx[None, :]
    intra = jnp.where(rel >= 0, jnp.exp(log_gamma[:, None, None] * jnp.maximum(rel, 0.0)), 0.0)
    q_decay = jnp.exp(log_gamma[:, None] * (idx + 1.0))[:, :, None]
    k_decay = jnp.exp(log_gamma[:, None] * (c - 1.0 - idx))[:, :, None]
    chunk_decay = jnp.broadcast_to(jnp.exp(log_gamma * c)[:, None, None], (heads, 1, LANES))
    row = lambda b, j: b * nc + j
    full3 = lambda b, j: (0, 0, 0)
    return pl.pallas_call(
        functools.partial(_retention_kernel, dk=dk, dv=dv),
        grid=(batch, nc),
        in_specs=[pl.BlockSpec((c, 1), lambda b, j: (row(b, j), 0)),
                  pl.BlockSpec((1, half), lambda b, j: (0, 0)),
                  pl.BlockSpec((c, heads * dk), lambda b, j: (row(b, j), 0)),
                  pl.BlockSpec((c, heads * dk), lambda b, j: (row(b, j), 1)),
                  pl.BlockSpec((c, heads * dv), lambda b, j: (row(b, j), 1)),
                  pl.BlockSpec((c, heads * dv), lambda b, j: (row(b, j), 2)),
                  pl.BlockSpec((heads, c, c), full3),
                  pl.BlockSpec((heads, c, 1), full3),
                  pl.BlockSpec((heads, c, 1), full3),
                  pl.BlockSpec((heads, 1, LANES), full3),
                  pl.BlockSpec((1, heads * dv), lambda b, j: (0, 0))],
        out_specs=pl.BlockSpec((c, heads * dv), lambda b, j: (row(b, j), 0)),
        out_shape=jax.ShapeDtypeStruct((n, heads * dv), BF16),
        scratch_shapes=[pltpu.VMEM((heads, dk, dv), F32)],
        compiler_params=_cparams("arbitrary", "arbitrary"),
        name="retention_core",
    )(positions.reshape(n, 1), inv, proj, proj, proj, proj, intra, q_decay, k_decay,
      chunk_decay, out_gain.reshape(1, heads * dv))


def _conv_kernel(u_ref, halo_ref, w_ref, b_ref, lg_ref, lb_ref, o_ref, ext, conv, *, tile):
    first = pl.program_id(1) == 0
    ext[0, 0:CONV_HALO, :] = jnp.where(first, 0.0, halo_ref[...])
    ext[0, CONV_HALO:CONV_HALO + tile, :] = u_ref[...]
    span = ext.shape[1] - SUBLANES
    for s in range(1, SUBLANES):
        ext[s, 0:span, :] = ext[0, s:s + span, :]
    base = CONV_HALO - (CONV_WIDTH - 1)
    ch = u_ref.shape[1]
    cw = min(ch, max(LANES, CONV_ACC_VREGS * SUBLANES // tile * LANES))
    for c0 in range(0, ch, cw):
        part = jnp.zeros((tile, cw), F32) + b_ref[:, c0:c0 + cw]
        for j in range(CONV_WIDTH):
            shift, start = (base + j) % SUBLANES, (base + j) // SUBLANES * SUBLANES
            part = part + ext[shift, start:start + tile, c0:c0 + cw] * w_ref[j:j + 1, c0:c0 + cw]
        conv[:, c0:c0 + cw] = part
    acc = conv[...]
    mu = jnp.mean(acc, axis=-1, keepdims=True)
    cen = acc - mu
    var = jnp.mean(cen * cen, axis=-1, keepdims=True)
    y = cen * lax.rsqrt(var + NORM_EPS) * lg_ref[...] + lb_ref[...]
    o_ref[...] = _silu(y).astype(o_ref.dtype)


def _conv_ln_silu(u, dw_w, dw_b, ln_g, ln_b, batch, seq):
    n, ch = u.shape
    t = _tile(seq, 128)
    assert t % CONV_HALO == 0
    nt = seq // t
    halo_per_tile = t // CONV_HALO
    row = lambda b, j: b * nt + j
    vec = pl.BlockSpec((1, ch), lambda b, j: (0, 0))
    return pl.pallas_call(
        functools.partial(_conv_kernel, tile=t),
        grid=(batch, nt),
        in_specs=[pl.BlockSpec((t, ch), lambda b, j: (row(b, j), 0)),
                  pl.BlockSpec((CONV_HALO, ch),
                               lambda b, j: (jnp.maximum(row(b, j) * halo_per_tile - 1, 0), 0)),
                  pl.BlockSpec((CONV_WIDTH, ch), lambda b, j: (0, 0)),
                  vec, vec, vec],
        out_specs=pl.BlockSpec((t, ch), lambda b, j: (row(b, j), 0)),
        out_shape=jax.ShapeDtypeStruct((n, ch), BF16),
        scratch_shapes=[pltpu.VMEM((SUBLANES, CONV_HALO + t, ch), F32),
                        pltpu.VMEM((t, ch), F32)],
        compiler_params=_cparams("arbitrary", "arbitrary"),
        name="conv_ln_silu",
    )(u, u, dw_w, dw_b.reshape(1, ch), ln_g.reshape(1, ch), ln_b.reshape(1, ch))


def _outproj_kernel(y_ref, w_ref, x_ref, mod_ref, g_ref, rw_ref, x1_ref, h2_ref, lg_ref, acc):
    k = pl.program_id(1)

    @pl.when(k == 0)
    def _():
        acc[...] = jnp.zeros_like(acc)

    acc[...] += jnp.dot(y_ref[...], w_ref[...], preferred_element_type=F32)

    @pl.when(k == pl.num_programs(1) - 1)
    def _():
        m = mod_ref[0]
        x1 = x_ref[...] + m[2:3] * acc[...]
        x1_ref[...] = x1
        h2 = _normmod(x1, g_ref[...], m[3:4], m[4:5])
        _store_row_tiles(h2_ref, 0, _pack_rows(h2))
        lg_ref[...] = _dot3(rw_ref[...], h2, _NT)


def _out_projection(y, w, x, modb, g_ffn, router_wt, seq):
    n, kdim = y.shape
    d = x.shape[1]
    ne = router_wt.shape[0]
    assert d // 2 == ROW_SUB * LANES
    tm = _tile(seq, 512)
    tk = _tile(kdim, 1024)
    tiles_per_seq = seq // tm
    return pl.pallas_call(
        _outproj_kernel,
        grid=(n // tm, kdim // tk),
        in_specs=[pl.BlockSpec((tm, tk), lambda i, k: (i, k)),
                  pl.BlockSpec((tk, d), lambda i, k: (k, 0)),
                  pl.BlockSpec((tm, d), lambda i, k: (i, 0)),
                  pl.BlockSpec((1, MOD_ROWS, d), lambda i, k: (i // tiles_per_seq, 0, 0)),
                  pl.BlockSpec((1, d), lambda i, k: (0, 0)),
                  pl.BlockSpec((ne, d), lambda i, k: (0, 0))],
        out_specs=[pl.BlockSpec((tm, d), lambda i, k: (i, 0)),
                   pl.BlockSpec((tm * ROW_SUB, LANES), lambda i, k: (i, 0)),
                   pl.BlockSpec((ne, tm), lambda i, k: (0, i))],
        out_shape=[jax.ShapeDtypeStruct((n, d), F32),
                   jax.ShapeDtypeStruct((n * ROW_SUB, LANES), U32),
                   jax.ShapeDtypeStruct((ne, n), F32)],
        scratch_shapes=[pltpu.VMEM((tm, d), F32)],
        compiler_params=_cparams("arbitrary", "arbitrary"),
        name="out_projection",
    )(y, w, x, modb, g_ffn, router_wt)


def _routing_kernel(lg_ref, bias_ref, eidx_ref, wts_ref):
    neg = -jnp.inf
    per = N_GROUPS
    tr = lg_ref.shape[1]
    scores = jax.nn.sigmoid(lg_ref[...])
    biased = scores + bias_ref[...]
    n_slab = lg_ref.shape[0] // per
    s_j = [scores[per * j:per * (j + 1), :] for j in range(n_slab)]
    b_j = [biased[per * j:per * (j + 1), :] for j in range(n_slab)]

    m1 = functools.reduce(jnp.maximum, b_j)
    ties = functools.reduce(lambda a, b: a + b, [(b == m1).astype(F32) for b in b_j])
    below = functools.reduce(jnp.maximum, [jnp.where(b < m1, b, neg) for b in b_j])
    grp = m1 + jnp.where(ties >= 2.0, m1, below)

    gid = lax.broadcasted_iota(I32, (per, tr), 0).astype(F32)
    chosen = jnp.zeros((per, tr), F32)
    cur = grp
    for _ in range(TOPK_GROUPS):
        mx = jnp.max(cur, axis=0, keepdims=True)
        first = jnp.min(jnp.where(cur == mx, gid, float(per)), axis=0, keepdims=True)
        pick = gid == first
        chosen = jnp.where(pick, 1.0, chosen)
        cur = jnp.where(pick, neg, cur)

    eid_j = [gid * float(n_slab) + float(j) for j in range(n_slab)]
    cur_j = [jnp.where(chosen > 0.0, b, neg) for b in b_j]
    picked_scores = []
    for k in range(TOP_K):
        mx = jnp.max(functools.reduce(jnp.maximum, cur_j), axis=0, keepdims=True)
        cand = functools.reduce(
            jnp.minimum, [jnp.where(c == mx, e, float(per * n_slab)) for c, e in zip(cur_j, eid_j)])
        first = jnp.min(cand, axis=0, keepdims=True)
        pick_j = [e == first for e in eid_j]
        sc = functools.reduce(lambda a, b: a + b,
                              [jnp.where(p, s, 0.0) for p, s in zip(pick_j, s_j)])
        picked_scores.append(jnp.sum(sc, axis=0, keepdims=True))
        cur_j = [jnp.where(p, neg, c) for p, c in zip(pick_j, cur_j)]
        eidx_ref[k:k + 1, :] = first.astype(I32)
    total = functools.reduce(lambda a, b: a + b, picked_scores)
    for k in range(TOP_K):
        wts_ref[k:k + 1, :] = picked_scores[k] / total * ROUTED_SCALE


def _routing(logits_t, bias_perm):
    ne, n = logits_t.shape
    tr = _tile(n, 1024)
    return pl.pallas_call(
        _routing_kernel,
        grid=(n // tr,),
        in_specs=[pl.BlockSpec((ne, tr), lambda i: (0, i)),
                  pl.BlockSpec((ne, 1), lambda i: (0, 0))],
        out_specs=[pl.BlockSpec((TOP_K, tr), lambda i: (0, i)),
                   pl.BlockSpec((TOP_K, tr), lambda i: (0, i))],
        out_shape=[jax.ShapeDtypeStruct((TOP_K, n), I32),
                   jax.ShapeDtypeStruct((TOP_K, n), F32)],
        compiler_params=_cparams("arbitrary"),
        name="moe_routing",
    )(logits_t, bias_perm.reshape(ne, 1))


def _plan_kernel(e_ref, u_ref, l_ref, dest_ref, be_ref, nu_ref, ss_ref, sn_ref, cnt, base, carry,
                 *, tb, ne):
    p = pl.program_id(0)
    t = pl.program_id(1)
    tp = e_ref.shape[1]
    ei = e_ref[...]
    eid = lax.broadcasted_iota(I32, (ne, tp), 0)
    hit = [ei[k:k + 1, :] == eid for k in range(TOP_K)]
    onehot = functools.reduce(lambda a, b: a + b, [h.astype(F32) for h in hit])
    tile_cnt = jnp.sum(onehot, axis=1, keepdims=True)

    @pl.when((p == 0) & (t == 0))
    def _():
        cnt[...] = jnp.zeros_like(cnt)

    @pl.when(p == 0)
    def _():
        cnt[...] += tile_cnt

    @pl.when((p == 1) & (t == 0))
    def _():
        nblk = jnp.floor((cnt[...] + float(tb - 1)) * (1.0 / tb))
        start_blk = jnp.dot(l_ref[...], nblk.astype(BF16), preferred_element_type=F32)
        base[...] = start_blk * float(tb)
        carry[...] = jnp.zeros_like(carry)
        end_blk = start_blk + nblk
        nbp = be_ref.shape[1]
        blk = lax.broadcasted_iota(I32, (ne, nbp), 1).astype(F32)
        owner = jnp.sum((end_blk[:, :1] <= blk).astype(F32), axis=0, keepdims=True)
        be_ref[...] = jnp.minimum(owner, float(ne - 1)).astype(I32)
        nu_ref[...] = end_blk[ne - 1:ne, :].astype(I32)
        diag = (lax.broadcasted_iota(I32, (ne, LANES), 0)
                == lax.broadcasted_iota(I32, (ne, LANES), 1))
        ss_ref[...] = jnp.sum(jnp.where(diag, start_blk, 0.0), axis=0, keepdims=True).astype(I32)
        sn_ref[...] = jnp.sum(jnp.where(diag, nblk, 0.0), axis=0, keepdims=True).astype(I32)

    @pl.when(p == 1)
    def _():
        before = jnp.dot(onehot.astype(BF16), u_ref[...], preferred_element_type=F32)
        rowpos = base[:, :1] + carry[:, :1] + before
        tok = dest_ref.shape[2]
        for k in range(TOP_K):
            row = jnp.sum(jnp.where(hit[k], rowpos, 0.0), axis=0, keepdims=True).astype(I32)
            for c in range(dest_ref.shape[0]):
                dest_ref[c, k:k + 1, :] = row[:, c * tok:(c + 1) * tok]
        carry[...] += tile_cnt


def _dispatch_plan(eidx, ne, tb, n_blocks, tok_tile):
    n = eidx.shape[1]
    tp = _tile(n, 2 * tok_tile)
    assert n // tb + 1 <= 256
    assert ne <= LANES
    lane_row = pl.BlockSpec((1, LANES), lambda p, t: (0, 0))
    nbp = -(-n_blocks // LANES) * LANES
    upper = (jnp.arange(tp)[:, None] < jnp.arange(tp)[None, :]).astype(BF16)
    lower = (jnp.arange(ne)[None, :] < jnp.arange(ne)[:, None]).astype(BF16)
    return pl.pallas_call(
        functools.partial(_plan_kernel, tb=tb, ne=ne),
        grid=(2, n // tp),
        in_specs=[pl.BlockSpec((TOP_K, tp), lambda p, t: (0, t)),
                  pl.BlockSpec((tp, tp), lambda p, t: (0, 0)),
                  pl.BlockSpec((ne, ne), lambda p, t: (0, 0))],
        out_specs=[pl.BlockSpec((tp // tok_tile, TOP_K, tok_tile), lambda p, t: (t * p, 0, 0)),
                   pl.BlockSpec((1, nbp), lambda p, t: (0, 0)),
                   lane_row, lane_row, lane_row],
        out_shape=[jax.ShapeDtypeStruct((n // tok_tile, TOP_K, tok_tile), I32),
                   jax.ShapeDtypeStruct((1, nbp), I32)]
        + [jax.ShapeDtypeStruct((1, LANES), I32)] * 3,
        scratch_shapes=[pltpu.VMEM((ne, LANES), F32)] * 3,
        compiler_params=_cparams("arbitrary", "arbitrary"),
        name="moe_dispatch_plan",
    )(eidx, upper, lower)


def _rows(ref, row, count=1):
    return ref.at[pl.ds(pl.multiple_of(row * ROW_SUB, ROW_SUB), count * ROW_SUB), :]


def _row_copy(src, s_row, dst, d_row, sem):
    return pltpu.make_async_copy(_rows(src, s_row), _rows(dst, d_row), sem)


def _dispatch_kernel(ss_ref, sn_ref, dest_ref, h_ref, xs_hbm, zeros, sem, zsem, *, td, tb, ne):
    @pl.when(pl.program_id(0) == 0)
    def _():
        zeros[...] = jnp.zeros_like(zeros)

        def fill_copy(blk):
            return pltpu.make_async_copy(zeros, _rows(xs_hbm, blk * tb, tb), zsem)

        def fill(e, carry):
            @pl.when(sn_ref[e] > 0)
            def _():
                fill_copy(ss_ref[e] + sn_ref[e] - 1).start()
            return carry

        def filled(e, carry):
            @pl.when(sn_ref[e] > 0)
            def _():
                fill_copy(0).wait()
            return carry

        lax.fori_loop(0, ne, fill, 0)
        lax.fori_loop(0, ne, filled, 0)

    def issue(t, carry):
        for k in range(TOP_K):
            _row_copy(h_ref, t, xs_hbm, dest_ref[0, k, t], sem).start(priority=k % DMA_THREADS)
        return carry

    lax.fori_loop(0, td, issue, 0)

    def drain(t, carry):
        for _ in range(TOP_K):
            _row_copy(h_ref, 0, xs_hbm, 0, sem).wait()
        return carry

    lax.fori_loop(0, td, drain, 0)


def _dispatch(dest3, h2p, seg_start, seg_nblk, n_rows, tb, ne):
    nt, _, td = dest3.shape
    grid_spec = pltpu.PrefetchScalarGridSpec(
        num_scalar_prefetch=2,
        grid=(nt,),
        in_specs=[pl.BlockSpec((1, TOP_K, td), lambda i, ss, sn: (i, 0, 0),
                               memory_space=pltpu.SMEM),
                  pl.BlockSpec((td * ROW_SUB, LANES), lambda i, ss, sn: (i, 0))],
        out_specs=pl.BlockSpec(memory_space=pl.ANY),
        scratch_shapes=[pltpu.VMEM((tb * ROW_SUB, LANES), U32),
                        pltpu.SemaphoreType.DMA(()), pltpu.SemaphoreType.DMA(())])
    return pl.pallas_call(
        functools.partial(_dispatch_kernel, td=td, tb=tb, ne=ne),
        grid_spec=grid_spec,
        out_shape=jax.ShapeDtypeStruct((n_rows * ROW_SUB, LANES), U32),
        compiler_params=_cparams("arbitrary"),
        name="moe_row_dispatch",
    )(seg_start, seg_nblk, dest3, h2p)


def _expert_kernel(be_ref, nu_ref, sn_ref, x_ref, wgu_hbm, wdn_hbm, o_ref, gu_f32, dn_f32, wgu_b,
                   wdn_b, sem, *, layer, ne):
    i = pl.program_id(0)
    live = i < nu_ref[0]
    e = be_ref[i]
    new_expert = (i == 0) | (e != be_ref[jnp.maximum(i - 1, 0)])

    def weight_copies(expert):
        return (pltpu.make_async_copy(wgu_hbm.at[layer, expert], gu_f32, sem.at[0]),
                pltpu.make_async_copy(wdn_hbm.at[layer, expert], dn_f32, sem.at[1]))

    @pl.when(live & (i == 0))
    def _():
        for cp in weight_copies(e):
            cp.start()

    @pl.when(live & new_expert)
    def _():
        for cp in weight_copies(e):
            cp.wait()
        wgu_b[...] = gu_f32[...].astype(BF16)
        wdn_b[...] = dn_f32[...].astype(BF16)
        nxt = lax.while_loop(lambda j: (j < ne) & (sn_ref[jnp.minimum(j, ne - 1)] == 0),
                             lambda j: j + 1, e + 1)

        @pl.when(nxt < ne)
        def _():
            for cp in weight_copies(nxt):
                cp.start()

    @pl.when(live)
    def _():
        ff = wdn_b.shape[0]
        tb = x_ref.shape[0] // ROW_SUB
        x = _load_row_tiles_bf16(x_ref, 0, tb)
        gu = jnp.dot(x, wgu_b[...], preferred_element_type=F32)
        act = (_silu(gu[:, :ff]) * gu[:, ff:]).astype(BF16)
        _store_row_tiles(o_ref, 0, _pack_rows(jnp.dot(act, wdn_b[...], preferred_element_type=F32)))


def _expert_ffn(xs, block_e, n_used, seg_nblk, w_gu, w_dn, layer, tb):
    _, ne, d, ff2 = w_gu.shape
    ff = w_dn.shape[2]
    nb = xs.shape[0] // (tb * ROW_SUB)
    row_block = pl.BlockSpec((tb * ROW_SUB, LANES),
                             lambda i, be, nu, sn: (jnp.minimum(i, nu[0] - 1), 0))
    grid_spec = pltpu.PrefetchScalarGridSpec(
        num_scalar_prefetch=3,
        grid=(nb,),
        in_specs=[row_block, pl.BlockSpec(memory_space=pl.ANY), pl.BlockSpec(memory_space=pl.ANY)],
        out_specs=row_block,
        scratch_shapes=[pltpu.VMEM((d, ff2), F32), pltpu.VMEM((ff, d), F32),
                        pltpu.VMEM((d, ff2), BF16), pltpu.VMEM((ff, d), BF16),
                        pltpu.SemaphoreType.DMA((2,))])
    return pl.pallas_call(
        functools.partial(_expert_kernel, layer=layer, ne=ne),
        grid_spec=grid_spec,
        out_shape=jax.ShapeDtypeStruct(xs.shape, U32),
        compiler_params=_cparams("arbitrary"),
        name="moe_expert_ffn",
    )(block_e, n_used, seg_nblk, xs, w_gu, w_dn)


def _combine_kernel(dest_ref, next_ref, wts_ref, ys_hbm, h_ref, x_ref, mod_ref, sgu_ref, sdn_ref,
                    fn_ref, o_ref, gbuf, shared, sem, *, tc, final):
    i = pl.program_id(0)
    last = i == pl.num_programs(0) - 1
    slot_rows = TOP_K * tc
    slot = i % 2
    other = 1 - slot

    def start_rows(d_ref, t, to_slot):
        for k in range(TOP_K):
            _row_copy(ys_hbm, d_ref[0, k, t], gbuf, to_slot * slot_rows + k * tc + t,
                      sem.at[to_slot]).start(priority=k % DMA_THREADS)

    def drain(of_slot):
        def body(t, carry):
            for _ in range(TOP_K):
                _row_copy(ys_hbm, 0, gbuf, 0, sem.at[of_slot]).wait()
            return carry

        lax.fori_loop(0, tc, body, 0)

    @pl.when(i == 0)
    def _():
        def body(t, carry):
            start_rows(dest_ref, t, 0)
            return carry

        lax.fori_loop(0, tc, body, 0)

    ff = sdn_ref.shape[0]
    gu = jnp.dot(_load_row_tiles_bf16(h_ref, 0, tc), sgu_ref[...], preferred_element_type=F32)
    act = (_silu(gu[:, :ff]) * gu[:, ff:]).astype(BF16)
    shared[...] = jnp.dot(act, sdn_ref[...], preferred_element_type=F32)

    drain(slot)

    base = slot * slot_rows * ROW_SUB
    gate = mod_ref[0][5:6]

    def group(rg, carry):
        r0 = pl.multiple_of(rg * COMBINE_GROUP, COMBINE_GROUP)
        for t in range(COMBINE_GROUP):
            start_rows(next_ref, r0 + t, other)
        w = wts_ref[pl.ds(r0, COMBINE_GROUP), :]
        lo, hi = _load_row_tiles(gbuf, base + r0 * ROW_SUB, COMBINE_GROUP)
        acc_lo, acc_hi = w[:, 0:1] * lo, w[:, 0:1] * hi
        for k in range(1, TOP_K):
            lo, hi = _load_row_tiles(gbuf, base + (k * tc + r0) * ROW_SUB, COMBINE_GROUP)
            acc_lo = acc_lo + w[:, k:k + 1] * lo
            acc_hi = acc_hi + w[:, k:k + 1] * hi
        moe = jnp.concatenate([acc_lo, acc_hi], axis=1) + shared[pl.ds(r0, COMBINE_GROUP), :]
        x2 = x_ref[pl.ds(r0, COMBINE_GROUP), :] + gate * moe
        if final:
            x2 = x2 * lax.rsqrt(jnp.mean(x2 * x2, axis=-1, keepdims=True) + NORM_EPS) * fn_ref[...]
        o_ref[pl.ds(r0, COMBINE_GROUP), :] = x2
        return carry

    lax.fori_loop(0, tc // COMBINE_GROUP, group, 0)

    @pl.when(last)
    def _():
        drain(other)


def _combine(dest3, wts_t, ys, h2p, x1, modb, s_gu, s_dn, final_norm, seq, final):
    n, d = x1.shape
    nt, _, tc = dest3.shape
    tiles_per_seq = seq // tc
    full2 = lambda i: (0, 0)
    return pl.pallas_call(
        functools.partial(_combine_kernel, tc=tc, final=final),
        grid=(nt,),
        in_specs=[pl.BlockSpec((1, TOP_K, tc), lambda i: (i, 0, 0), memory_space=pltpu.SMEM),
                  pl.BlockSpec((1, TOP_K, tc), lambda i: (jnp.minimum(i + 1, nt - 1), 0, 0),
                               memory_space=pltpu.SMEM),
                  pl.BlockSpec((tc, TOP_K), lambda i: (i, 0)),
                  pl.BlockSpec(memory_space=pl.ANY),
                  pl.BlockSpec((tc * ROW_SUB, LANES), lambda i: (i, 0)),
                  pl.BlockSpec((tc, d), lambda i: (i, 0)),
                  pl.BlockSpec((1, MOD_ROWS, d), lambda i: (i // tiles_per_seq, 0, 0)),
                  pl.BlockSpec(s_gu.shape, full2),
                  pl.BlockSpec(s_dn.shape, full2),
                  pl.BlockSpec((1, d), full2)],
        out_specs=pl.BlockSpec((tc, d), lambda i: (i, 0)),
        out_shape=jax.ShapeDtypeStruct((n, d), F32),
        scratch_shapes=[pltpu.VMEM((2 * TOP_K * tc * ROW_SUB, LANES), U32),
                        pltpu.VMEM((tc, d), F32),
                        pltpu.SemaphoreType.DMA((2,))],
        compiler_params=_cparams("arbitrary"),
        name="moe_combine",
    )(dest3, dest3, wts_t, ys, h2p, x1, modb, s_gu, s_dn, final_norm.reshape(1, d))


def _moe(x1, h2p, logits_t, modb, bias_perm, w_gu, w_dn, layer, s_gu, s_dn, final_norm, seq,
         final):
    n, d = x1.shape
    ne = w_gu.shape[1]
    tb = EXPERT_BLOCK_ROWS
    tok_tile = _tile(seq, 256)
    n_blocks = n * TOP_K // tb + ne
    eidx, wts = _routing(logits_t, bias_perm)
    dest3, block_e, n_used, seg_start, seg_nblk = _dispatch_plan(eidx, ne, tb, n_blocks, tok_tile)
    xs = _dispatch(dest3, h2p, seg_start[0, :ne], seg_nblk[0, :ne], n_blocks * tb, tb, ne)
    ys = _expert_ffn(xs, block_e[0, :n_blocks], n_used[0, :1], seg_nblk[0, :ne], w_gu, w_dn,
                     layer, tb)
    return _combine(dest3, wts.T, ys, h2p, x1, modb, s_gu, s_dn, final_norm, seq, final)


def _expert_major_rows(a):
    per = a.shape[0] // N_GROUPS
    return a.reshape((N_GROUPS, per) + a.shape[1:]).swapaxes(0, 1).reshape(a.shape)


def _pairs_to_halves(w, heads):
    d, cols = w.shape
    dk = cols // heads
    return w.reshape(d, heads, dk // 2, 2).transpose(0, 1, 3, 2).reshape(d, cols)


def kernel(x, c, positions, mod_w, mod_b, norm_mix, norm_ffn, ret_w_in, ret_w_out, ret_out_gain, conv_w_in, conv_dw_w, conv_dw_b, conv_ln_g, conv_ln_b, conv_w_out, router_w, router_bias, exp_w_gu, exp_w_down, shared_w_gu, shared_w_down, final_norm):
    batch, seq, d = x.shape
    n = batch * seq
    depth = mod_w.shape[0]
    heads = RET_HEADS
    qk_cols = d
    mods = _modulation(c, mod_w, mod_b)
    xt = x.reshape(n, d)
    for i in range(depth):
        modb = _mod_block(mods[i], d)
        g_mix = norm_mix[i].reshape(1, d)
        g_ffn = norm_ffn[i].reshape(1, d)
        j = i // 2
        if i % 2 == 0:
            w_in = ret_w_in[j]
            w_in = jnp.concatenate([_pairs_to_halves(w_in[:, :qk_cols], heads),
                                    _pairs_to_halves(w_in[:, qk_cols:2 * qk_cols], heads),
                                    w_in[:, 2 * qk_cols:]], axis=1).astype(BF16)
            proj = _normmod_proj(xt, g_mix, modb, w_in, seq, glu=False)
            y = _retention_core(proj, positions, ret_out_gain[j], batch, seq, d)
            w_out = ret_w_out[j].astype(BF16)
        else:
            u = _normmod_proj(xt, g_mix, modb, conv_w_in[j].astype(BF16), seq, glu=True)
            y = _conv_ln_silu(u, conv_dw_w[j], conv_dw_b[j], conv_ln_g[j], conv_ln_b[j], batch, seq)
            w_out = conv_w_out[j].astype(BF16)
        router_wt = _expert_major_rows(router_w[i].T)
        x1, h2p, logits_t = _out_projection(y, w_out, xt, modb, g_ffn, router_wt, seq)
        xt = _moe(x1, h2p, logits_t, modb, _expert_major_rows(router_bias[i]),
                  exp_w_gu, exp_w_down, i,
                  shared_w_gu[i].astype(BF16), shared_w_down[i].astype(BF16),
                  final_norm, seq, final=(i == depth - 1))
    return xt.reshape(batch, seq, d)
```

```python
import functools

import jax
import jax.numpy as jnp
from jax import lax
from jax.experimental import pallas as pl
from jax.experimental.pallas import tpu as pltpu

F32 = jnp.float32
BF16 = jnp.bfloat16
I32 = jnp.int32
U32 = jnp.uint32

RET_HEADS = 8
ROPE_BASE = 10000.0
CONV_WIDTH = 31
CONV_HALO = 32
CONV_ACC_VREGS = 32
N_GROUPS = 8
TOPK_GROUPS = 4
TOP_K = 8
ROUTED_SCALE = 2.5
NORM_EPS = 1e-6
MOD_ROWS = 8
LANES = 128
SUBLANES = 8
ROW_SUB = 8
COMBINE_GROUP = 16
DMA_THREADS = 2
EXPERT_BLOCK_ROWS = 512
VMEM_LIMIT = 56 * 1024 * 1024


def _cparams(*sem):
    return pltpu.CompilerParams(dimension_semantics=sem, vmem_limit_bytes=VMEM_LIMIT)


def _tile(n, pref):
    t = min(n, pref)
    assert n % t == 0, (n, pref)
    return t


def _split_bf16(a):
    hi = a.astype(BF16)
    lo = (a - hi.astype(F32)).astype(BF16)
    return hi, lo


def _dot3(a, b, dims):
    ah, al = _split_bf16(a)
    bh, bl = _split_bf16(b)
    dg = functools.partial(lax.dot_general, dimension_numbers=dims, preferred_element_type=F32)
    return dg(ah, bh) + dg(ah, bl) + dg(al, bh)


_NN = (((1,), (0,)), ((), ()))
_NT = (((1,), (1,)), ((), ()))
_TN = (((0,), (0,)), ((), ()))


def _normmod(x, g, shift, scale):
    y = x * lax.rsqrt(jnp.mean(x * x, axis=-1, keepdims=True) + NORM_EPS)
    return (y * g) * (1.0 + scale) + shift


def _silu(x):
    return x * jax.nn.sigmoid(x)


def _pack_rows(h):
    half = h.shape[1] // 2
    bits = lax.bitcast_convert_type(h.astype(BF16).astype(F32), U32)
    lo = lax.shift_right_logical(bits[:, :half], jnp.uint32(16))
    hi = bits[:, half:] & jnp.uint32(0xFFFF0000)
    return hi | lo


def _unpack_rows(w):
    lo = lax.bitcast_convert_type(lax.shift_left(w, jnp.uint32(16)), F32)
    hi = lax.bitcast_convert_type(w & jnp.uint32(0xFFFF0000), F32)
    return lo, hi


def _store_row_tiles(ref, base, packed):
    t = packed.shape[0]
    assert packed.shape[1] == ROW_SUB * LANES
    for s in range(ROW_SUB):
        ref[pl.ds(base + s, t, stride=ROW_SUB), :] = packed[:, s * LANES:(s + 1) * LANES]


def _load_row_tiles(ref, base, t):
    parts = [_unpack_rows(ref[pl.ds(base + s, t, stride=ROW_SUB), :]) for s in range(ROW_SUB)]
    lo = jnp.concatenate([p[0] for p in parts], axis=1)
    hi = jnp.concatenate([p[1] for p in parts], axis=1)
    return lo, hi


def _load_row_tiles_bf16(ref, base, t):
    lo, hi = _load_row_tiles(ref, base, t)
    return jnp.concatenate([lo.astype(BF16), hi.astype(BF16)], axis=1)


def _mod_kernel(c_ref, w_ref, b_ref, o_ref):
    c = c_ref[...]
    o_ref[0] = _dot3(_silu(c), w_ref[0], _NN) + b_ref[0]


def _modulation(c, mod_w, mod_b):
    depth, d, n6 = mod_w.shape
    b = c.shape[0]
    assert b <= MOD_ROWS
    c_pad = jnp.zeros((MOD_ROWS, d), F32).at[:b].set(c)
    tn = _tile(n6, 1024)
    out = pl.pallas_call(
        _mod_kernel,
        grid=(depth, n6 // tn),
        in_specs=[pl.BlockSpec((MOD_ROWS, d), lambda i, j: (0, 0)),
                  pl.BlockSpec((1, d, tn), lambda i, j: (i, 0, j)),
                  pl.BlockSpec((1, 1, tn), lambda i, j: (i, 0, j))],
        out_specs=pl.BlockSpec((1, MOD_ROWS, tn), lambda i, j: (i, 0, j)),
        out_shape=jax.ShapeDtypeStruct((depth, MOD_ROWS, n6), F32),
        compiler_params=_cparams("arbitrary", "arbitrary"),
        name="adaln_modulation",
    )(c_pad, mod_w, mod_b.reshape(depth, 1, n6))
    return out[:, :b]


def _mod_block(mod_i, d):
    b = mod_i.shape[0]
    m = mod_i.reshape(b, 6, d)
    return jnp.concatenate([m, jnp.zeros((b, MOD_ROWS - 6, d), F32)], axis=1)


def _proj_kernel(x_ref, g_ref, mod_ref, w_ref, o_ref, h_scr):
    @pl.when(pl.program_id(1) == 0)
    def _():
        m = mod_ref[0]
        h_scr[...] = _normmod(x_ref[...], g_ref[...], m[0:1], m[1:2]).astype(BF16)

    o_ref[...] = jnp.dot(h_scr[...], w_ref[...], preferred_element_type=F32).astype(o_ref.dtype)


def _glu_proj_kernel(x_ref, g_ref, mod_ref, wa_ref, wb_ref, o_ref, h_scr):
    @pl.when(pl.program_id(1) == 0)
    def _():
        m = mod_ref[0]
        h_scr[...] = _normmod(x_ref[...], g_ref[...], m[0:1], m[1:2]).astype(BF16)

    h = h_scr[...]
    a = jnp.dot(h, wa_ref[...], preferred_element_type=F32)
    b = jnp.dot(h, wb_ref[...], preferred_element_type=F32)
    o_ref[...] = a * jax.nn.sigmoid(b)


def _normmod_proj(x, g, modb, w, seq, glu):
    n, d = x.shape
    nout = w.shape[1] // 2 if glu else w.shape[1]
    tm = _tile(seq, 1024)
    tn = _tile(nout, 512 if glu else 1024)
    tiles_per_seq = seq // tm
    x_spec = pl.BlockSpec((tm, d), lambda i, j: (i, 0))
    g_spec = pl.BlockSpec((1, d), lambda i, j: (0, 0))
    m_spec = pl.BlockSpec((1, MOD_ROWS, d), lambda i, j: (i // tiles_per_seq, 0, 0))
    if glu:
        half_blocks = nout // tn
        in_specs = [x_spec, g_spec, m_spec,
                    pl.BlockSpec((d, tn), lambda i, j: (0, j)),
                    pl.BlockSpec((d, tn), lambda i, j: (0, j + half_blocks))]
        body, args, odt = _glu_proj_kernel, (x, g, modb, w, w), F32
    else:
        in_specs = [x_spec, g_spec, m_spec, pl.BlockSpec((d, tn), lambda i, j: (0, j))]
        body, args, odt = _proj_kernel, (x, g, modb, w), BF16
    return pl.pallas_call(
        body,
        grid=(n // tm, nout // tn),
        in_specs=in_specs,
        out_specs=pl.BlockSpec((tm, tn), lambda i, j: (i, j)),
        out_shape=jax.ShapeDtypeStruct((n, nout), odt),
        scratch_shapes=[pltpu.VMEM((tm, d), BF16)],
        compiler_params=_cparams("arbitrary", "arbitrary"),
        name="glu_in_projection" if glu else "in_projection",
    )(*args)


def _retention_kernel(pos_ref, inv_ref, q_ref, k_ref, v_ref, g_ref, intra_ref, qd_ref, kd_ref,
                      cd_ref, gain_ref, o_ref, state, *, dk, dv):
    @pl.when(pl.program_id(1) == 0)
    def _():
        state[...] = jnp.zeros_like(state)

    half = dk // 2
    ang = pos_ref[...].astype(F32) * inv_ref[...]
    cos = jnp.cos(ang)
    sin = jnp.sin(ang)

    def rot(ref, h):
        x1 = ref[:, h * dk:h * dk + half].astype(F32)
        x2 = ref[:, h * dk + half:(h + 1) * dk].astype(F32)
        return jnp.concatenate([x1 * cos - x2 * sin, x1 * sin + x2 * cos], axis=1)

    for h in range(RET_HEADS):
        q = rot(q_ref, h)
        k = rot(k_ref, h) * (dk ** -0.5)
        v = v_ref[:, h * dv:(h + 1) * dv]
        s = lax.dot_general(q.astype(BF16), k.astype(BF16), _NT, preferred_element_type=F32)
        p = (s * intra_ref[h]).astype(BF16)
        inner = jnp.dot(p, v, preferred_element_type=F32)
        st = state[h]
        cross = jnp.dot((q * qd_ref[h]).astype(BF16), st.astype(BF16),
                        preferred_element_type=F32)
        kv = lax.dot_general((k * kd_ref[h]).astype(BF16), v, _TN, preferred_element_type=F32)
        state[h] = cd_ref[h][:, :1] * st + kv
        o = inner + cross
        o = o * lax.rsqrt(jnp.mean(o * o, axis=-1, keepdims=True) + NORM_EPS)
        gate = g_ref[:, h * dv:(h + 1) * dv].astype(F32)
        o_ref[:, h * dv:(h + 1) * dv] = (_silu(gate) * (o * gain_ref[:, h * dv:(h + 1) * dv])
                                         ).astype(o_ref.dtype)


def _retention_core(proj, positions, out_gain, batch, seq, d):
    n = batch * seq
    heads, dk = RET_HEADS, d // RET_HEADS
    dv = 2 * dk
    c = _tile(seq, 128)
    nc = seq // c
    half = dk // 2
    inv = (1.0 / (ROPE_BASE ** jnp.linspace(0.0, 1.0, half, dtype=F32))).reshape(1, half)
    log_gamma = jnp.log1p(-jnp.exp2(-5.0 - jnp.arange(heads, dtype=F32)))
    idx = jnp.arange(c, dtype=F32)
    rel = idx[:, None] - idx[None, :]
    intra = jnp.where(rel >= 0, jnp.exp(log_gamma[:, None, None] * jnp.maximum(rel, 0.0)), 0.0)
    q_decay = jnp.exp(log_gamma[:, None] * (idx + 1.0))[:, :, None]
    k_decay = jnp.exp(log_gamma[:, None] * (c - 1.0 - idx))[:, :, None]
    chunk_decay = jnp.broadcast_to(jnp.exp(log_gamma * c)[:, None, None], (heads, 1, LANES))
    row = lambda b, j: b * nc + j
    full3 = lambda b, j: (0, 0, 0)
    return pl.pallas_call(
        functools.partial(_retention_kernel, dk=dk, dv=dv),
        grid=(batch, nc),
        in_specs=[pl.BlockSpec((c, 1), lambda b, j: (row(b, j), 0)),
                  pl.BlockSpec((1, half), lambda b, j: (0, 0)),
                  pl.BlockSpec((c, heads * dk), lambda b, j: (row(b, j), 0)),
                  pl.BlockSpec((c, heads * dk), lambda b, j: (row(b, j), 1)),
                  pl.BlockSpec((c, heads * dv), lambda b, j: (row(b, j), 1)),
                  pl.BlockSpec((c, heads * dv), lambda b, j: (row(b, j), 2)),
                  pl.BlockSpec((heads, c, c), full3),
                  pl.BlockSpec((heads, c, 1), full3),
                  pl.BlockSpec((heads, c, 1), full3),
                  pl.BlockSpec((heads, 1, LANES), full3),
                  pl.BlockSpec((1, heads * dv), lambda b, j: (0, 0))],
        out_specs=pl.BlockSpec((c, heads * dv), lambda b, j: (row(b, j), 0)),
        out_shape=jax.ShapeDtypeStruct((n, heads * dv), BF16),
        scratch_shapes=[pltpu.VMEM((heads, dk, dv), F32)],
        compiler_params=_cparams("arbitrary", "arbitrary"),
        name="retention_core",
    )(positions.reshape(n, 1), inv, proj, proj, proj, proj, intra, q_decay, k_decay,
      chunk_decay, out_gain.reshape(1, heads * dv))


def _conv_kernel(u_ref, halo_ref, w_ref, b_ref, lg_ref, lb_ref, o_ref, ext, conv, *, tile):
    first = pl.program_id(1) == 0
    ext[0, 0:CONV_HALO, :] = jnp.where(first, 0.0, halo_ref[...])
    ext[0, CONV_HALO:CONV_HALO + tile, :] = u_ref[...]
    span = ext.shape[1] - SUBLANES
    for s in range(1, SUBLANES):
        ext[s, 0:span, :] = ext[0, s:s + span, :]
    base = CONV_HALO - (CONV_WIDTH - 1)
    ch = u_ref.shape[1]
    cw = min(ch, max(LANES, CONV_ACC_VREGS * SUBLANES // tile * LANES))
    for c0 in range(0, ch, cw):
        part = jnp.zeros((tile, cw), F32) + b_ref[:, c0:c0 + cw]
        for j in range(CONV_WIDTH):
            shift, start = (base + j) % SUBLANES, (base + j) // SUBLANES * SUBLANES
            part = part + ext[shift, start:start + tile, c0:c0 + cw] * w_ref[j:j + 1, c0:c0 + cw]
        conv[:, c0:c0 + cw] = part
    acc = conv[...]
    mu = jnp.mean(acc, axis=-1, keepdims=True)
    cen = acc - mu
    var = jnp.mean(cen * cen, axis=-1, keepdims=True)
    y = cen * lax.rsqrt(var + NORM_EPS) * lg_ref[...] + lb_ref[...]
    o_ref[...] = _silu(y).astype(o_ref.dtype)


def _conv_ln_silu(u, dw_w, dw_b, ln_g, ln_b, batch, seq):
    n, ch = u.shape
    t = _tile(seq, 128)
    assert t % CONV_HALO == 0
    nt = seq // t
    halo_per_tile = t // CONV_HALO
    row = lambda b, j: b * nt + j
    vec = pl.BlockSpec((1, ch), lambda b, j: (0, 0))
    return pl.pallas_call(
        functools.partial(_conv_kernel, tile=t),
        grid=(batch, nt),
        in_specs=[pl.BlockSpec((t, ch), lambda b, j: (row(b, j), 0)),
                  pl.BlockSpec((CONV_HALO, ch),
                               lambda b, j: (jnp.maximum(row(b, j) * halo_per_tile - 1, 0), 0)),
                  pl.BlockSpec((CONV_WIDTH, ch), lambda b, j: (0, 0)),
                  vec, vec, vec],
        out_specs=pl.BlockSpec((t, ch), lambda b, j: (row(b, j), 0)),
        out_shape=jax.ShapeDtypeStruct((n, ch), BF16),
        scratch_shapes=[pltpu.VMEM((SUBLANES, CONV_HALO + t, ch), F32),
                        pltpu.VMEM((t, ch), F32)],
        compiler_params=_cparams("arbitrary", "arbitrary"),
        name="conv_ln_silu",
    )(u, u, dw_w, dw_b.reshape(1, ch), ln_g.reshape(1, ch), ln_b.reshape(1, ch))


def _outproj_kernel(y_ref, w_ref, x_ref, mod_ref, g_ref, rw_ref, x1_ref, h2_ref, lg_ref, acc):
    k = pl.program_id(1)

    @pl.when(k == 0)
    def _():
        acc[...] = jnp.zeros_like(acc)

    acc[...] += jnp.dot(y_ref[...], w_ref[...], preferred_element_type=F32)

    @pl.when(k == pl.num_programs(1) - 1)
    def _():
        m = mod_ref[0]
        x1 = x_ref[...] + m[2:3] * acc[...]
        x1_ref[...] = x1
        h2 = _normmod(x1, g_ref[...], m[3:4], m[4:5])
        _store_row_tiles(h2_ref, 0, _pack_rows(h2))
        lg_ref[...] = _dot3(rw_ref[...], h2, _NT)


def _out_projection(y, w, x, modb, g_ffn, router_wt, seq):
    n, kdim = y.shape
    d = x.shape[1]
    ne = router_wt.shape[0]
    assert d // 2 == ROW_SUB * LANES
    tm = _tile(seq, 512)
    tk = _tile(kdim, 1024)
    tiles_per_seq = seq // tm
    return pl.pallas_call(
        _outproj_kernel,
        grid=(n // tm, kdim // tk),
        in_specs=[pl.BlockSpec((tm, tk), lambda i, k: (i, k)),
                  pl.BlockSpec((tk, d), lambda i, k: (k, 0)),
                  pl.BlockSpec((tm, d), lambda i, k: (i, 0)),
                  pl.BlockSpec((1, MOD_ROWS, d), lambda i, k: (i // tiles_per_seq, 0, 0)),
                  pl.BlockSpec((1, d), lambda i, k: (0, 0)),
                  pl.BlockSpec((ne, d), lambda i, k: (0, 0))],
        out_specs=[pl.BlockSpec((tm, d), lambda i, k: (i, 0)),
                   pl.BlockSpec((tm * ROW_SUB, LANES), lambda i, k: (i, 0)),
                   pl.BlockSpec((ne, tm), lambda i, k: (0, i))],
        out_shape=[jax.ShapeDtypeStruct((n, d), F32),
                   jax.ShapeDtypeStruct((n * ROW_SUB, LANES), U32),
                   jax.ShapeDtypeStruct((ne, n), F32)],
        scratch_shapes=[pltpu.VMEM((tm, d), F32)],
        compiler_params=_cparams("arbitrary", "arbitrary"),
        name="out_projection",
    )(y, w, x, modb, g_ffn, router_wt)


def _routing_kernel(lg_ref, bias_ref, eidx_ref, wts_ref):
    neg = -jnp.inf
    per = N_GROUPS
    tr = lg_ref.shape[1]
    scores = jax.nn.sigmoid(lg_ref[...])
    biased = scores + bias_ref[...]
    n_slab = lg_ref.shape[0] // per
    s_j = [scores[per * j:per * (j + 1), :] for j in range(n_slab)]
    b_j = [biased[per * j:per * (j + 1), :] for j in range(n_slab)]

    m1 = functools.reduce(jnp.maximum, b_j)
    ties = functools.reduce(lambda a, b: a + b, [(b == m1).astype(F32) for b in b_j])
    below = functools.reduce(jnp.maximum, [jnp.where(b < m1, b, neg) for b in b_j])
    grp = m1 + jnp.where(ties >= 2.0, m1, below)

    gid = lax.broadcasted_iota(I32, (per, tr), 0).astype(F32)
    chosen = jnp.zeros((per, tr), F32)
    cur = grp
    for _ in range(TOPK_GROUPS):
        mx = jnp.max(cur, axis=0, keepdims=True)
        first = jnp.min(jnp.where(cur == mx, gid, float(per)), axis=0, keepdims=True)
        pick = gid == first
        chosen = jnp.where(pick, 1.0, chosen)
        cur = jnp.where(pick, neg, cur)

    eid_j = [gid * float(n_slab) + float(j) for j in range(n_slab)]
    cur_j = [jnp.where(chosen > 0.0, b, neg) for b in b_j]
    picked_scores = []
    for k in range(TOP_K):
        mx = jnp.max(functools.reduce(jnp.maximum, cur_j), axis=0, keepdims=True)
        cand = functools.reduce(
            jnp.minimum, [jnp.where(c == mx, e, float(per * n_slab)) for c, e in zip(cur_j, eid_j)])
        first = jnp.min(cand, axis=0, keepdims=True)
        pick_j = [e == first for e in eid_j]
        sc = functools.reduce(lambda a, b: a + b,
                              [jnp.where(p, s, 0.0) for p, s in zip(pick_j, s_j)])
        picked_scores.append(jnp.sum(sc, axis=0, keepdims=True))
        cur_j = [jnp.where(p, neg, c) for p, c in zip(pick_j, cur_j)]
        eidx_ref[k:k + 1, :] = first.astype(I32)
    total = functools.reduce(lambda a, b: a + b, picked_scores)
    for k in range(TOP_K):
        wts_ref[k:k + 1, :] = picked_scores[k] / total * ROUTED_SCALE


def _routing(logits_t, bias_perm):
    ne, n = logits_t.shape
    tr = _tile(n, 1024)
    return pl.pallas_call(
        _routing_kernel,
        grid=(n // tr,),
        in_specs=[pl.BlockSpec((ne, tr), lambda i: (0, i)),
                  pl.BlockSpec((ne, 1), lambda i: (0, 0))],
        out_specs=[pl.BlockSpec((TOP_K, tr), lambda i: (0, i)),
                   pl.BlockSpec((TOP_K, tr), lambda i: (0, i))],
        out_shape=[jax.ShapeDtypeStruct((TOP_K, n), I32),
                   jax.ShapeDtypeStruct((TOP_K, n), F32)],
        compiler_params=_cparams("arbitrary"),
        name="moe_routing",
    )(logits_t, bias_perm.reshape(ne, 1))


def _plan_kernel(e_ref, u_ref, l_ref, dest_ref, be_ref, nu_ref, ss_ref, sn_ref, cnt, base, carry,
                 *, tb, ne):
    p = pl.program_id(0)
    t = pl.program_id(1)
    tp = e_ref.shape[1]
    ei = e_ref[...]
    eid = lax.broadcasted_iota(I32, (ne, tp), 0)
    hit = [ei[k:k + 1, :] == eid for k in range(TOP_K)]
    onehot = functools.reduce(lambda a, b: a + b, [h.astype(F32) for h in hit])
    tile_cnt = jnp.sum(onehot, axis=1, keepdims=True)

    @pl.when((p == 0) & (t == 0))
    def _():
        cnt[...] = jnp.zeros_like(cnt)

    @pl.when(p == 0)
    def _():
        cnt[...] += tile_cnt

    @pl.when((p == 1) & (t == 0))
    def _():
        nblk = jnp.floor((cnt[...] + float(tb - 1)) * (1.0 / tb))
        start_blk = jnp.dot(l_ref[...], nblk.astype(BF16), preferred_element_type=F32)
        base[...] = start_blk * float(tb)
        carry[...] = jnp.zeros_like(carry)
        end_blk = start_blk + nblk
        nbp = be_ref.shape[1]
        blk = lax.broadcasted_iota(I32, (ne, nbp), 1).astype(F32)
        owner = jnp.sum((end_blk[:, :1] <= blk).astype(F32), axis=0, keepdims=True)
        be_ref[...] = jnp.minimum(owner, float(ne - 1)).astype(I32)
        nu_ref[...] = end_blk[ne - 1:ne, :].astype(I32)
        diag = (lax.broadcasted_iota(I32, (ne, LANES), 0)
                == lax.broadcasted_iota(I32, (ne, LANES), 1))
        ss_ref[...] = jnp.sum(jnp.where(diag, start_blk, 0.0), axis=0, keepdims=True).astype(I32)
        sn_ref[...] = jnp.sum(jnp.where(diag, nblk, 0.0), axis=0, keepdims=True).astype(I32)

    @pl.when(p == 1)
    def _():
        before = jnp.dot(onehot.astype(BF16), u_ref[...], preferred_element_type=F32)
        rowpos = base[:, :1] + carry[:, :1] + before
        tok = dest_ref.shape[2]
        for k in range(TOP_K):
            row = jnp.sum(jnp.where(hit[k], rowpos, 0.0), axis=0, keepdims=True).astype(I32)
            for c in range(dest_ref.shape[0]):
                dest_ref[c, k:k + 1, :] = row[:, c * tok:(c + 1) * tok]
        carry[...] += tile_cnt


def _dispatch_plan(eidx, ne, tb, n_blocks, tok_tile):
    n = eidx.shape[1]
    tp = _tile(n, 2 * tok_tile)
    assert n // tb + 1 <= 256
    assert ne <= LANES
    lane_row = pl.BlockSpec((1, LANES), lambda p, t: (0, 0))
    nbp = -(-n_blocks // LANES) * LANES
    upper = (jnp.arange(tp)[:, None] < jnp.arange(tp)[None, :]).astype(BF16)
    lower = (jnp.arange(ne)[None, :] < jnp.arange(ne)[:, None]).astype(BF16)
    return pl.pallas_call(
        functools.partial(_plan_kernel, tb=tb, ne=ne),
        grid=(2, n // tp),
        in_specs=[pl.BlockSpec((TOP_K, tp), lambda p, t: (0, t)),
                  pl.BlockSpec((tp, tp), lambda p, t: (0, 0)),
                  pl.BlockSpec((ne, ne), lambda p, t: (0, 0))],
        out_specs=[pl.BlockSpec((tp // tok_tile, TOP_K, tok_tile), lambda p, t: (t * p, 0, 0)),
                   pl.BlockSpec((1, nbp), lambda p, t: (0, 0)),
                   lane_row, lane_row, lane_row],
        out_shape=[jax.ShapeDtypeStruct((n // tok_tile, TOP_K, tok_tile), I32),
                   jax.ShapeDtypeStruct((1, nbp), I32)]
        + [jax.ShapeDtypeStruct((1, LANES), I32)] * 3,
        scratch_shapes=[pltpu.VMEM((ne, LANES), F32)] * 3,
        compiler_params=_cparams("arbitrary", "arbitrary"),
        name="moe_dispatch_plan",
    )(eidx, upper, lower)


def _rows(ref, row, count=1):
    return ref.at[pl.ds(pl.multiple_of(row * ROW_SUB, ROW_SUB), count * ROW_SUB), :]


def _row_copy(src, s_row, dst, d_row, sem):
    return pltpu.make_async_copy(_rows(src, s_row), _rows(dst, d_row), sem)


def _dispatch_kernel(ss_ref, sn_ref, dest_ref, h_ref, sgu_ref, sdn_ref, xs_hbm, sh_ref, zeros, sem,
                     zsem, *, td, tb, ne):
    @pl.when(pl.program_id(0) == 0)
    def _():
        zeros[...] = jnp.zeros_like(zeros)

        def fill_copy(blk):
            return pltpu.make_async_copy(zeros, _rows(xs_hbm, blk * tb, tb), zsem)

        def fill(e, carry):
            @pl.when(sn_ref[e] > 0)
            def _():
                fill_copy(ss_ref[e] + sn_ref[e] - 1).start()
            return carry

        def filled(e, carry):
            @pl.when(sn_ref[e] > 0)
            def _():
                fill_copy(0).wait()
            return carry

        lax.fori_loop(0, ne, fill, 0)
        lax.fori_loop(0, ne, filled, 0)

    def issue(t, carry):
        for k in range(TOP_K):
            _row_copy(h_ref, t, xs_hbm, dest_ref[0, k, t], sem).start(priority=k % DMA_THREADS)
        return carry

    lax.fori_loop(0, td, issue, 0)

    ff = sdn_ref.shape[0]
    gu = jnp.dot(_load_row_tiles_bf16(h_ref, 0, td), sgu_ref[...], preferred_element_type=F32)
    act = (_silu(gu[:, :ff]) * gu[:, ff:]).astype(BF16)
    _store_row_tiles(sh_ref, 0, _pack_rows(jnp.dot(act, sdn_ref[...], preferred_element_type=F32)))

    def drain(t, carry):
        for _ in range(TOP_K):
            _row_copy(h_ref, 0, xs_hbm, 0, sem).wait()
        return carry

    lax.fori_loop(0, td, drain, 0)


def _dispatch(dest3, h2p, seg_start, seg_nblk, s_gu, s_dn, n_rows, tb, ne):
    nt, _, td = dest3.shape
    tile = pl.BlockSpec((td * ROW_SUB, LANES), lambda i, ss, sn: (i, 0))
    grid_spec = pltpu.PrefetchScalarGridSpec(
        num_scalar_prefetch=2,
        grid=(nt,),
        in_specs=[pl.BlockSpec((1, TOP_K, td), lambda i, ss, sn: (i, 0, 0),
                               memory_space=pltpu.SMEM),
                  tile,
                  pl.BlockSpec(s_gu.shape, lambda i, ss, sn: (0, 0)),
                  pl.BlockSpec(s_dn.shape, lambda i, ss, sn: (0, 0))],
        out_specs=[pl.BlockSpec(memory_space=pl.ANY), tile],
        scratch_shapes=[pltpu.VMEM((tb * ROW_SUB, LANES), U32),
                        pltpu.SemaphoreType.DMA(()), pltpu.SemaphoreType.DMA(())])
    return pl.pallas_call(
        functools.partial(_dispatch_kernel, td=td, tb=tb, ne=ne),
        grid_spec=grid_spec,
        out_shape=[jax.ShapeDtypeStruct((n_rows * ROW_SUB, LANES), U32),
                   jax.ShapeDtypeStruct(h2p.shape, U32)],
        compiler_params=_cparams("arbitrary"),
        name="moe_row_dispatch",
    )(seg_start, seg_nblk, dest3, h2p, s_gu, s_dn)


def _expert_kernel(be_ref, nu_ref, sn_ref, x_ref, wgu_hbm, wdn_hbm, o_ref, gu_f32, dn_f32, wgu_b,
                   wdn_b, sem, *, layer, ne):
    i = pl.program_id(0)
    live = i < nu_ref[0]
    e = be_ref[i]
    new_expert = (i == 0) | (e != be_ref[jnp.maximum(i - 1, 0)])

    def weight_copies(expert):
        return (pltpu.make_async_copy(wgu_hbm.at[layer, expert], gu_f32, sem.at[0]),
                pltpu.make_async_copy(wdn_hbm.at[layer, expert], dn_f32, sem.at[1]))

    @pl.when(live & (i == 0))
    def _():
        for cp in weight_copies(e):
            cp.start()

    @pl.when(live & new_expert)
    def _():
        for cp in weight_copies(e):
            cp.wait()
        wgu_b[...] = gu_f32[...].astype(BF16)
        wdn_b[...] = dn_f32[...].astype(BF16)
        nxt = lax.while_loop(lambda j: (j < ne) & (sn_ref[jnp.minimum(j, ne - 1)] == 0),
                             lambda j: j + 1, e + 1)

        @pl.when(nxt < ne)
        def _():
            for cp in weight_copies(nxt):
                cp.start()

    @pl.when(live)
    def _():
        ff = wdn_b.shape[0]
        tb = x_ref.shape[0] // ROW_SUB
        x = _load_row_tiles_bf16(x_ref, 0, tb)
        gu = jnp.dot(x, wgu_b[...], preferred_element_type=F32)
        act = (_silu(gu[:, :ff]) * gu[:, ff:]).astype(BF16)
        _store_row_tiles(o_ref, 0, _pack_rows(jnp.dot(act, wdn_b[...], preferred_element_type=F32)))


def _expert_ffn(xs, block_e, n_used, seg_nblk, w_gu, w_dn, layer, tb):
    _, ne, d, ff2 = w_gu.shape
    ff = w_dn.shape[2]
    nb = xs.shape[0] // (tb * ROW_SUB)
    row_block = pl.BlockSpec((tb * ROW_SUB, LANES),
                             lambda i, be, nu, sn: (jnp.minimum(i, nu[0] - 1), 0))
    grid_spec = pltpu.PrefetchScalarGridSpec(
        num_scalar_prefetch=3,
        grid=(nb,),
        in_specs=[row_block, pl.BlockSpec(memory_space=pl.ANY), pl.BlockSpec(memory_space=pl.ANY)],
        out_specs=row_block,
        scratch_shapes=[pltpu.VMEM((d, ff2), F32), pltpu.VMEM((ff, d), F32),
                        pltpu.VMEM((d, ff2), BF16), pltpu.VMEM((ff, d), BF16),
                        pltpu.SemaphoreType.DMA((2,))])
    return pl.pallas_call(
        functools.partial(_expert_kernel, layer=layer, ne=ne),
        grid_spec=grid_spec,
        out_shape=jax.ShapeDtypeStruct(xs.shape, U32),
        compiler_params=_cparams("arbitrary"),
        name="moe_expert_ffn",
    )(block_e, n_used, seg_nblk, xs, w_gu, w_dn)


def _combine_kernel(dest_ref, next_ref, wts_ref, ys_hbm, sh_ref, x_ref, mod_ref, fn_ref, o_ref, gbuf,
                    sem, *, tc, final):
    i = pl.program_id(0)
    last = i == pl.num_programs(0) - 1
    slot_rows = TOP_K * tc
    slot = i % 2
    other = 1 - slot

    def start_rows(d_ref, t, to_slot):
        for k in range(TOP_K):
            _row_copy(ys_hbm, d_ref[0, k, t], gbuf, to_slot * slot_rows + k * tc + t,
                      sem.at[to_slot]).start(priority=k % DMA_THREADS)

    def drain(of_slot):
        def body(t, carry):
            for _ in range(TOP_K):
                _row_copy(ys_hbm, 0, gbuf, 0, sem.at[of_slot]).wait()
            return carry

        lax.fori_loop(0, tc, body, 0)

    @pl.when(i == 0)
    def _():
        def body(t, carry):
            start_rows(dest_ref, t, 0)
            return carry

        lax.fori_loop(0, tc, body, 0)

    drain(slot)

    base = slot * slot_rows * ROW_SUB
    gate = mod_ref[0][5:6]

    def group(rg, carry):
        r0 = pl.multiple_of(rg * COMBINE_GROUP, COMBINE_GROUP)
        w = wts_ref[pl.ds(r0, COMBINE_GROUP), :]
        per_k = COMBINE_GROUP // TOP_K
        acc_lo = acc_hi = None
        for k in range(TOP_K):
            for t in range(k * per_k, (k + 1) * per_k):
                start_rows(next_ref, r0 + t, other)
            lo, hi = _load_row_tiles(gbuf, base + (k * tc + r0) * ROW_SUB, COMBINE_GROUP)
            acc_lo = w[:, k:k + 1] * lo if k == 0 else acc_lo + w[:, k:k + 1] * lo
            acc_hi = w[:, k:k + 1] * hi if k == 0 else acc_hi + w[:, k:k + 1] * hi
        lo, hi = _load_row_tiles(sh_ref, r0 * ROW_SUB, COMBINE_GROUP)
        moe = jnp.concatenate([acc_lo + lo, acc_hi + hi], axis=1)
        x2 = x_ref[pl.ds(r0, COMBINE_GROUP), :] + gate * moe
        if final:
            x2 = x2 * lax.rsqrt(jnp.mean(x2 * x2, axis=-1, keepdims=True) + NORM_EPS) * fn_ref[...]
        o_ref[pl.ds(r0, COMBINE_GROUP), :] = x2
        return carry

    lax.fori_loop(0, tc // COMBINE_GROUP, group, 0)

    @pl.when(last)
    def _():
        drain(other)


def _combine(dest3, wts_t, ys, shared_rows, x1, modb, final_norm, seq, final):
    n, d = x1.shape
    nt, _, tc = dest3.shape
    tiles_per_seq = seq // tc
    full2 = lambda i: (0, 0)
    return pl.pallas_call(
        functools.partial(_combine_kernel, tc=tc, final=final),
        grid=(nt,),
        in_specs=[pl.BlockSpec((1, TOP_K, tc), lambda i: (i, 0, 0), memory_space=pltpu.SMEM),
                  pl.BlockSpec((1, TOP_K, tc), lambda i: (jnp.minimum(i + 1, nt - 1), 0, 0),
                               memory_space=pltpu.SMEM),
                  pl.BlockSpec((tc, TOP_K), lambda i: (i, 0)),
                  pl.BlockSpec(memory_space=pl.ANY),
                  pl.BlockSpec((tc * ROW_SUB, LANES), lambda i: (i, 0)),
                  pl.BlockSpec((tc, d), lambda i: (i, 0)),
                  pl.BlockSpec((1, MOD_ROWS, d), lambda i: (i // tiles_per_seq, 0, 0)),
                  pl.BlockSpec((1, d), full2)],
        out_specs=pl.BlockSpec((tc, d), lambda i: (i, 0)),
        out_shape=jax.ShapeDtypeStruct((n, d), F32),
        scratch_shapes=[pltpu.VMEM((2 * TOP_K * tc * ROW_SUB, LANES), U32),
                        pltpu.SemaphoreType.DMA((2,))],
        compiler_params=_cparams("arbitrary"),
        name="moe_combine",
    )(dest3, dest3, wts_t, ys, shared_rows, x1, modb, final_norm.reshape(1, d))


def _moe(x1, h2p, logits_t, modb, bias_perm, w_gu, w_dn, layer, s_gu, s_dn, final_norm, seq,
         final):
    n, d = x1.shape
    ne = w_gu.shape[1]
    tb = EXPERT_BLOCK_ROWS
    tok_tile = _tile(seq, 256)
    n_blocks = n * TOP_K // tb + ne
    eidx, wts = _routing(logits_t, bias_perm)
    dest3, block_e, n_used, seg_start, seg_nblk = _dispatch_plan(eidx, ne, tb, n_blocks, tok_tile)
    xs, shared_rows = _dispatch(dest3, h2p, seg_start[0, :ne], seg_nblk[0, :ne], s_gu, s_dn,
                                n_blocks * tb, tb, ne)
    ys = _expert_ffn(xs, block_e[0, :n_blocks], n_used[0, :1], seg_nblk[0, :ne], w_gu, w_dn,
                     layer, tb)
    return _combine(dest3, wts.T, ys, shared_rows, x1, modb, final_norm, seq, final)


def _expert_major_rows(a):
    per = a.shape[0] // N_GROUPS
    return a.reshape((N_GROUPS, per) + a.shape[1:]).swapaxes(0, 1).reshape(a.shape)


def _pairs_to_halves(w, heads):
    d, cols = w.shape
    dk = cols // heads
    return w.reshape(d, heads, dk // 2, 2).transpose(0, 1, 3, 2).reshape(d, cols)


def kernel(x, c, positions, mod_w, mod_b, norm_mix, norm_ffn, ret_w_in, ret_w_out, ret_out_gain, conv_w_in, conv_dw_w, conv_dw_b, conv_ln_g, conv_ln_b, conv_w_out, router_w, router_bias, exp_w_gu, exp_w_down, shared_w_gu, shared_w_down, final_norm):
    batch, seq, d = x.shape
    n = batch * seq
    depth = mod_w.shape[0]
    heads = RET_HEADS
    qk_cols = d
    mods = _modulation(c, mod_w, mod_b)
    xt = x.reshape(n, d)
    for i in range(depth):
        modb = _mod_block(mods[i], d)
        g_mix = norm_mix[i].reshape(1, d)
        g_ffn = norm_ffn[i].reshape(1, d)
        j = i // 2
        if i % 2 == 0:
            w_in = ret_w_in[j]
            w_in = jnp.concatenate([_pairs_to_halves(w_in[:, :qk_cols], heads),
                                    _pairs_to_halves(w_in[:, qk_cols:2 * qk_cols], heads),
                                    w_in[:, 2 * qk_cols:]], axis=1).astype(BF16)
            proj = _normmod_proj(xt, g_mix, modb, w_in, seq, glu=False)
            y = _retention_core(proj, positions, ret_out_gain[j], batch, seq, d)
            w_out = ret_w_out[j].astype(BF16)
        else:
            u = _normmod_proj(xt, g_mix, modb, conv_w_in[j].astype(BF16), seq, glu=True)
            y = _conv_ln_silu(u, conv_dw_w[j], conv_dw_b[j], conv_ln_g[j], conv_ln_b[j], batch, seq)
            w_out = conv_w_out[j].astype(BF16)
        router_wt = _expert_major_rows(router_w[i].T)
        x1, h2p, logits_t = _out_projection(y, w_out, xt, modb, g_ffn, router_wt, seq)
        xt = _moe(x1, h2p, logits_t, modb, _expert_major_rows(router_bias[i]),
                  exp_w_gu, exp_w_down, i,
                  shared_w_gu[i].astype(BF16), shared_w_down[i].astype(BF16),
                  final_norm, seq, final=(i == depth - 1))
    return xt.reshape(batch, seq, d)
```

```python
import functools

import jax
import jax.numpy as jnp
from jax import lax
from jax.experimental import pallas as pl
from jax.experimental.pallas import tpu as pltpu

F32 = jnp.float32
BF16 = jnp.bfloat16
I32 = jnp.int32
U32 = jnp.uint32

RET_HEADS = 8
ROPE_BASE = 10000.0
CONV_WIDTH = 31
CONV_HALO = 32
CONV_ACC_VREGS = 32
N_GROUPS = 8
TOPK_GROUPS = 4
TOP_K = 8
ROUTED_SCALE = 2.5
NORM_EPS = 1e-6
MOD_ROWS = 8
LANES = 128
SUBLANES = 8
ROW_SUB = 8
COMBINE_GROUP = 16
DMA_THREADS = 2
EXPERT_BLOCK_ROWS = 512
VMEM_LIMIT = 56 * 1024 * 1024


def _cparams(*sem):
    return pltpu.CompilerParams(dimension_semantics=sem, vmem_limit_bytes=VMEM_LIMIT)


def _tile(n, pref):
    t = min(n, pref)
    assert n % t == 0, (n, pref)
    return t


def _split_bf16(a):
    hi = a.astype(BF16)
    lo = (a - hi.astype(F32)).astype(BF16)
    return hi, lo


def _dot3(a, b, dims):
    ah, al = _split_bf16(a)
    bh, bl = _split_bf16(b)
    dg = functools.partial(lax.dot_general, dimension_numbers=dims, preferred_element_type=F32)
    return dg(ah, bh) + dg(ah, bl) + dg(al, bh)


_NN = (((1,), (0,)), ((), ()))
_NT = (((1,), (1,)), ((), ()))
_TN = (((0,), (0,)), ((), ()))


def _normmod(x, g, shift, scale):
    y = x * lax.rsqrt(jnp.mean(x * x, axis=-1, keepdims=True) + NORM_EPS)
    return (y * g) * (1.0 + scale) + shift


def _silu(x):
    return x * jax.nn.sigmoid(x)


def _pack_rows(h):
    half = h.shape[1] // 2
    bits = lax.bitcast_convert_type(h.astype(BF16).astype(F32), U32)
    lo = lax.shift_right_logical(bits[:, :half], jnp.uint32(16))
    hi = bits[:, half:] & jnp.uint32(0xFFFF0000)
    return hi | lo


def _unpack_rows(w):
    lo = lax.bitcast_convert_type(lax.shift_left(w, jnp.uint32(16)), F32)
    hi = lax.bitcast_convert_type(w & jnp.uint32(0xFFFF0000), F32)
    return lo, hi


def _store_row_tiles(ref, base, packed):
    t = packed.shape[0]
    assert packed.shape[1] == ROW_SUB * LANES
    for s in range(ROW_SUB):
        ref[pl.ds(base + s, t, stride=ROW_SUB), :] = packed[:, s * LANES:(s + 1) * LANES]


def _load_row_tiles(ref, base, t):
    parts = [_unpack_rows(ref[pl.ds(base + s, t, stride=ROW_SUB), :]) for s in range(ROW_SUB)]
    lo = jnp.concatenate([p[0] for p in parts], axis=1)
    hi = jnp.concatenate([p[1] for p in parts], axis=1)
    return lo, hi


def _load_row_tiles_bf16(ref, base, t):
    lo, hi = _load_row_tiles(ref, base, t)
    return jnp.concatenate([lo.astype(BF16), hi.astype(BF16)], axis=1)


def _mod_kernel(c_ref, w_ref, b_ref, o_ref):
    c = c_ref[...]
    o_ref[0] = _dot3(_silu(c), w_ref[0], _NN) + b_ref[0]


def _modulation(c, mod_w, mod_b):
    depth, d, n6 = mod_w.shape
    b = c.shape[0]
    assert b <= MOD_ROWS
    c_pad = jnp.zeros((MOD_ROWS, d), F32).at[:b].set(c)
    tn = _tile(n6, 1024)
    out = pl.pallas_call(
        _mod_kernel,
        grid=(depth, n6 // tn),
        in_specs=[pl.BlockSpec((MOD_ROWS, d), lambda i, j: (0, 0)),
                  pl.BlockSpec((1, d, tn), lambda i, j: (i, 0, j)),
                  pl.BlockSpec((1, 1, tn), lambda i, j: (i, 0, j))],
        out_specs=pl.BlockSpec((1, MOD_ROWS, tn), lambda i, j: (i, 0, j)),
        out_shape=jax.ShapeDtypeStruct((depth, MOD_ROWS, n6), F32),
        compiler_params=_cparams("arbitrary", "arbitrary"),
        name="adaln_modulation",
    )(c_pad, mod_w, mod_b.reshape(depth, 1, n6))
    return out[:, :b]


def _mod_block(mod_i, d):
    b = mod_i.shape[0]
    m = mod_i.reshape(b, 6, d)
    return jnp.concatenate([m, jnp.zeros((b, MOD_ROWS - 6, d), F32)], axis=1)


def _proj_kernel(x_ref, g_ref, mod_ref, w_ref, o_ref, h_scr):
    @pl.when(pl.program_id(1) == 0)
    def _():
        m = mod_ref[0]
        h_scr[...] = _normmod(x_ref[...], g_ref[...], m[0:1], m[1:2]).astype(BF16)

    o_ref[...] = jnp.dot(h_scr[...], w_ref[...], preferred_element_type=F32).astype(o_ref.dtype)


def _glu_proj_kernel(x_ref, g_ref, mod_ref, wa_ref, wb_ref, o_ref, h_scr):
    @pl.when(pl.program_id(1) == 0)
    def _():
        m = mod_ref[0]
        h_scr[...] = _normmod(x_ref[...], g_ref[...], m[0:1], m[1:2]).astype(BF16)

    h = h_scr[...]
    a = jnp.dot(h, wa_ref[...], preferred_element_type=F32)
    b = jnp.dot(h, wb_ref[...], preferred_element_type=F32)
    o_ref[...] = a * jax.nn.sigmoid(b)


def _normmod_proj(x, g, modb, w, seq, glu):
    n, d = x.shape
    nout = w.shape[1] // 2 if glu else w.shape[1]
    tm = _tile(seq, 1024)
    tn = _tile(nout, 512 if glu else 1024)
    tiles_per_seq = seq // tm
    x_spec = pl.BlockSpec((tm, d), lambda i, j: (i, 0))
    g_spec = pl.BlockSpec((1, d), lambda i, j: (0, 0))
    m_spec = pl.BlockSpec((1, MOD_ROWS, d), lambda i, j: (i // tiles_per_seq, 0, 0))
    if glu:
        half_blocks = nout // tn
        in_specs = [x_spec, g_spec, m_spec,
                    pl.BlockSpec((d, tn), lambda i, j: (0, j)),
                    pl.BlockSpec((d, tn), lambda i, j: (0, j + half_blocks))]
        body, args, odt = _glu_proj_kernel, (x, g, modb, w, w), F32
    else:
        in_specs = [x_spec, g_spec, m_spec, pl.BlockSpec((d, tn), lambda i, j: (0, j))]
        body, args, odt = _proj_kernel, (x, g, modb, w), BF16
    return pl.pallas_call(
        body,
        grid=(n // tm, nout // tn),
        in_specs=in_specs,
        out_specs=pl.BlockSpec((tm, tn), lambda i, j: (i, j)),
        out_shape=jax.ShapeDtypeStruct((n, nout), odt),
        scratch_shapes=[pltpu.VMEM((tm, d), BF16)],
        compiler_params=_cparams("arbitrary", "arbitrary"),
        name="glu_in_projection" if glu else "in_projection",
    )(*args)


def _retention_kernel(pos_ref, inv_ref, q_ref, k_ref, v_ref, g_ref, intra_ref, qd_ref, kd_ref,
                      cd_ref, gain_ref, o_ref, state, *, dk, dv):
    @pl.when(pl.program_id(1) == 0)
    def _():
        state[...] = jnp.zeros_like(state)

    half = dk // 2
    ang = pos_ref[...].astype(F32) * inv_ref[...]
    cos = jnp.cos(ang)
    sin = jnp.sin(ang)

    def rot(ref, h):
        x1 = ref[:, h * dk:h * dk + half].astype(F32)
        x2 = ref[:, h * dk + half:(h + 1) * dk].astype(F32)
        return jnp.concatenate([x1 * cos - x2 * sin, x1 * sin + x2 * cos], axis=1)

    for h in range(RET_HEADS):
        q = rot(q_ref, h)
        k = rot(k_ref, h) * (dk ** -0.5)
        v = v_ref[:, h * dv:(h + 1) * dv]
        s = lax.dot_general(q.astype(BF16), k.astype(BF16), _NT, preferred_element_type=F32)
        p = (s * intra_ref[h]).astype(BF16)
        inner = jnp.dot(p, v, preferred_element_type=F32)
        st = state[h]
        cross = jnp.dot((q * qd_ref[h]).astype(BF16), st.astype(BF16),
                        preferred_element_type=F32)
        kv = lax.dot_general((k * kd_ref[h]).astype(BF16), v, _TN, preferred_element_type=F32)
        state[h] = cd_ref[h][:, :1] * st + kv
        o = inner + cross
        o = o * lax.rsqrt(jnp.mean(o * o, axis=-1, keepdims=True) + NORM_EPS)
        gate = g_ref[:, h * dv:(h + 1) * dv].astype(F32)
        o_ref[:, h * dv:(h + 1) * dv] = (_silu(gate) * (o * gain_ref[:, h * dv:(h + 1) * dv])
                                         ).astype(o_ref.dtype)


def _retention_core(proj, positions, out_gain, batch, seq, d):
    n = batch * seq
    heads, dk = RET_HEADS, d // RET_HEADS
    dv = 2 * dk
    c = _tile(seq, 128)
    nc = seq // c
    half = dk // 2
    inv = (1.0 / (ROPE_BASE ** jnp.linspace(0.0, 1.0, half, dtype=F32))).reshape(1, half)
    log_gamma = jnp.log1p(-jnp.exp2(-5.0 - jnp.arange(heads, dtype=F32)))
    idx = jnp.arange(c, dtype=F32)
    rel = idx[:, None] - idx[None, :]
    intra = jnp.where(rel >= 0, jnp.exp(log_gamma[:, None, None] * jnp.maximum(rel, 0.0)), 0.0)
    q_decay = jnp.exp(log_gamma[:, None] * (idx + 1.0))[:, :, None]
    k_decay = jnp.exp(log_gamma[:, None] * (c - 1.0 - idx))[:, :, None]
    chunk_decay = jnp.broadcast_to(jnp.exp(log_gamma * c)[:, None, None], (heads, 1, LANES))
    row = lambda b, j: b * nc + j
    full3 = lambda b, j: (0, 0, 0)
    return pl.pallas_call(
        functools.partial(_retention_kernel, dk=dk, dv=dv),
        grid=(batch, nc),
        in_specs=[pl.BlockSpec((c, 1), lambda b, j: (row(b, j), 0)),
                  pl.BlockSpec((1, half), lambda b, j: (0, 0)),
                  pl.BlockSpec((c, heads * dk), lambda b, j: (row(b, j), 0)),
                  pl.BlockSpec((c, heads * dk), lambda b, j: (row(b, j), 1)),
                  pl.BlockSpec((c, heads * dv), lambda b, j: (row(b, j), 1)),
                  pl.BlockSpec((c, heads * dv), lambda b, j: (row(b, j), 2)),
                  pl.BlockSpec((heads, c, c), full3),
                  pl.BlockSpec((heads, c, 1), full3),
                  pl.BlockSpec((heads, c, 1), full3),
                  pl.BlockSpec((heads, 1, LANES), full3),
                  pl.BlockSpec((1, heads * dv), lambda b, j: (0, 0))],
        out_specs=pl.BlockSpec((c, heads * dv), lambda b, j: (row(b, j), 0)),
        out_shape=jax.ShapeDtypeStruct((n, heads * dv), BF16),
        scratch_shapes=[pltpu.VMEM((heads, dk, dv), F32)],
        compiler_params=_cparams("arbitrary", "arbitrary"),
        name="retention_core",
    )(positions.reshape(n, 1), inv, proj, proj, proj, proj, intra, q_decay, k_decay,
      chunk_decay, out_gain.reshape(1, heads * dv))


def _conv_kernel(u_ref, halo_ref, w_ref, b_ref, lg_ref, lb_ref, o_ref, ext, conv, *, tile):
    first = pl.program_id(1) == 0
    ext[0, 0:CONV_HALO, :] = jnp.where(first, 0.0, halo_ref[...])
    ext[0, CONV_HALO:CONV_HALO + tile, :] = u_ref[...]
    span = ext.shape[1] - SUBLANES
    for s in range(1, SUBLANES):
        ext[s, 0:span, :] = ext[0, s:s + span, :]
    base = CONV_HALO - (CONV_WIDTH - 1)
    ch = u_ref.shape[1]
    cw = min(ch, max(LANES, CONV_ACC_VREGS * SUBLANES // tile * LANES))
    for c0 in range(0, ch, cw):
        part = jnp.zeros((tile, cw), F32) + b_ref[:, c0:c0 + cw]
        for j in range(CONV_WIDTH):
            shift, start = (base + j) % SUBLANES, (base + j) // SUBLANES * SUBLANES
            part = part + ext[shift, start:start + tile, c0:c0 + cw] * w_ref[j:j + 1, c0:c0 + cw]
        conv[:, c0:c0 + cw] = part
    acc = conv[...]
    mu = jnp.mean(acc, axis=-1, keepdims=True)
    cen = acc - mu
    var = jnp.mean(cen * cen, axis=-1, keepdims=True)
    y = cen * lax.rsqrt(var + NORM_EPS) * lg_ref[...] + lb_ref[...]
    o_ref[...] = _silu(y).astype(o_ref.dtype)


def _conv_ln_silu(u, dw_w, dw_b, ln_g, ln_b, batch, seq):
    n, ch = u.shape
    t = _tile(seq, 128)
    assert t % CONV_HALO == 0
    nt = seq // t
    halo_per_tile = t // CONV_HALO
    row = lambda b, j: b * nt + j
    vec = pl.BlockSpec((1, ch), lambda b, j: (0, 0))
    return pl.pallas_call(
        functools.partial(_conv_kernel, tile=t),
        grid=(batch, nt),
        in_specs=[pl.BlockSpec((t, ch), lambda b, j: (row(b, j), 0)),
                  pl.BlockSpec((CONV_HALO, ch),
                               lambda b, j: (jnp.maximum(row(b, j) * halo_per_tile - 1, 0), 0)),
                  pl.BlockSpec((CONV_WIDTH, ch), lambda b, j: (0, 0)),
                  vec, vec, vec],
        out_specs=pl.BlockSpec((t, ch), lambda b, j: (row(b, j), 0)),
        out_shape=jax.ShapeDtypeStruct((n, ch), BF16),
        scratch_shapes=[pltpu.VMEM((SUBLANES, CONV_HALO + t, ch), F32),
                        pltpu.VMEM((t, ch), F32)],
        compiler_params=_cparams("arbitrary", "arbitrary"),
        name="conv_ln_silu",
    )(u, u, dw_w, dw_b.reshape(1, ch), ln_g.reshape(1, ch), ln_b.reshape(1, ch))


def _outproj_kernel(y_ref, w_ref, x_ref, mod_ref, g_ref, rw_ref, x1_ref, h2_ref, lg_ref, acc):
    k = pl.program_id(1)

    @pl.when(k == 0)
    def _():
        acc[...] = jnp.zeros_like(acc)

    acc[...] += jnp.dot(y_ref[...], w_ref[...], preferred_element_type=F32)

    @pl.when(k == pl.num_programs(1) - 1)
    def _():
        m = mod_ref[0]
        x1 = x_ref[...] + m[2:3] * acc[...]
        x1_ref[...] = x1
        h2 = _normmod(x1, g_ref[...], m[3:4], m[4:5])
        _store_row_tiles(h2_ref, 0, _pack_rows(h2))
        lg_ref[...] = _dot3(rw_ref[...], h2, _NT)


def _out_projection(y, w, x, modb, g_ffn, router_wt, seq):
    n, kdim = y.shape
    d = x.shape[1]
    ne = router_wt.shape[0]
    assert d // 2 == ROW_SUB * LANES
    tm = _tile(seq, 512)
    tk = _tile(kdim, 1024)
    tiles_per_seq = seq // tm
    return pl.pallas_call(
        _outproj_kernel,
        grid=(n // tm, kdim // tk),
        in_specs=[pl.BlockSpec((tm, tk), lambda i, k: (i, k)),
                  pl.BlockSpec((tk, d), lambda i, k: (k, 0)),
                  pl.BlockSpec((tm, d), lambda i, k: (i, 0)),
                  pl.BlockSpec((1, MOD_ROWS, d), lambda i, k: (i // tiles_per_seq, 0, 0)),
                  pl.BlockSpec((1, d), lambda i, k: (0, 0)),
                  pl.BlockSpec((ne, d), lambda i, k: (0, 0))],
        out_specs=[pl.BlockSpec((tm, d), lambda i, k: (i, 0)),
                   pl.BlockSpec((tm * ROW_SUB, LANES), lambda i, k: (i, 0)),
                   pl.BlockSpec((ne, tm), lambda i, k: (0, i))],
        out_shape=[jax.ShapeDtypeStruct((n, d), F32),
                   jax.ShapeDtypeStruct((n * ROW_SUB, LANES), U32),
                   jax.ShapeDtypeStruct((ne, n), F32)],
        scratch_shapes=[pltpu.VMEM((tm, d), F32)],
        compiler_params=_cparams("arbitrary", "arbitrary"),
        name="out_projection",
    )(y, w, x, modb, g_ffn, router_wt)


def _routing_kernel(lg_ref, bias_ref, eidx_ref, wts_ref):
    neg = -jnp.inf
    per = N_GROUPS
    tr = lg_ref.shape[1]
    scores = jax.nn.sigmoid(lg_ref[...])
    biased = scores + bias_ref[...]
    n_slab = lg_ref.shape[0] // per
    s_j = [scores[per * j:per * (j + 1), :] for j in range(n_slab)]
    b_j = [biased[per * j:per * (j + 1), :] for j in range(n_slab)]

    m1 = functools.reduce(jnp.maximum, b_j)
    ties = functools.reduce(lambda a, b: a + b, [(b == m1).astype(F32) for b in b_j])
    below = functools.reduce(jnp.maximum, [jnp.where(b < m1, b, neg) for b in b_j])
    grp = m1 + jnp.where(ties >= 2.0, m1, below)

    gid = lax.broadcasted_iota(I32, (per, tr), 0).astype(F32)
    chosen = jnp.zeros((per, tr), F32)
    cur = grp
    for _ in range(TOPK_GROUPS):
        mx = jnp.max(cur, axis=0, keepdims=True)
        first = jnp.min(jnp.where(cur == mx, gid, float(per)), axis=0, keepdims=True)
        pick = gid == first
        chosen = jnp.where(pick, 1.0, chosen)
        cur = jnp.where(pick, neg, cur)

    eid_j = [gid * float(n_slab) + float(j) for j in range(n_slab)]
    cur_j = [jnp.where(chosen > 0.0, b, neg) for b in b_j]
    picked_scores = []
    for k in range(TOP_K):
        mx = jnp.max(functools.reduce(jnp.maximum, cur_j), axis=0, keepdims=True)
        cand = functools.reduce(
            jnp.minimum, [jnp.where(c == mx, e, float(per * n_slab)) for c, e in zip(cur_j, eid_j)])
        first = jnp.min(cand, axis=0, keepdims=True)
        pick_j = [e == first for e in eid_j]
        sc = functools.reduce(lambda a, b: a + b,
                              [jnp.where(p, s, 0.0) for p, s in zip(pick_j, s_j)])
        picked_scores.append(jnp.sum(sc, axis=0, keepdims=True))
        cur_j = [jnp.where(p, neg, c) for p, c in zip(pick_j, cur_j)]
        eidx_ref[k:k + 1, :] = first.astype(I32)
    total = functools.reduce(lambda a, b: a + b, picked_scores)
    for k in range(TOP_K):
        wts_ref[k:k + 1, :] = picked_scores[k] / total * ROUTED_SCALE


def _routing(logits_t, bias_perm):
    ne, n = logits_t.shape
    tr = _tile(n, 1024)
    return pl.pallas_call(
        _routing_kernel,
        grid=(n // tr,),
        in_specs=[pl.BlockSpec((ne, tr), lambda i: (0, i)),
                  pl.BlockSpec((ne, 1), lambda i: (0, 0))],
        out_specs=[pl.BlockSpec((TOP_K, tr), lambda i: (0, i)),
                   pl.BlockSpec((TOP_K, tr), lambda i: (0, i))],
        out_shape=[jax.ShapeDtypeStruct((TOP_K, n), I32),
                   jax.ShapeDtypeStruct((TOP_K, n), F32)],
        compiler_params=_cparams("arbitrary"),
        name="moe_routing",
    )(logits_t, bias_perm.reshape(ne, 1))


def _plan_kernel(e_ref, u_ref, l_ref, dest_ref, be_ref, nu_ref, ss_ref, sn_ref, sc_ref, cnt, base,
                 carry, *, tb, ne):
    p = pl.program_id(0)
    t = pl.program_id(1)
    tp = e_ref.shape[1]
    ei = e_ref[...]
    eid = lax.broadcasted_iota(I32, (ne, tp), 0)
    hit = [ei[k:k + 1, :] == eid for k in range(TOP_K)]
    onehot = functools.reduce(lambda a, b: a + b, [h.astype(F32) for h in hit])
    tile_cnt = jnp.sum(onehot, axis=1, keepdims=True)

    @pl.when((p == 0) & (t == 0))
    def _():
        cnt[...] = jnp.zeros_like(cnt)

    @pl.when(p == 0)
    def _():
        cnt[...] += tile_cnt

    @pl.when((p == 1) & (t == 0))
    def _():
        nblk = jnp.floor((cnt[...] + float(tb - 1)) * (1.0 / tb))
        start_blk = jnp.dot(l_ref[...], nblk.astype(BF16), preferred_element_type=F32)
        base[...] = start_blk * float(tb)
        carry[...] = jnp.zeros_like(carry)
        end_blk = start_blk + nblk
        nbp = be_ref.shape[1]
        blk = lax.broadcasted_iota(I32, (ne, nbp), 1).astype(F32)
        owner = jnp.sum((end_blk[:, :1] <= blk).astype(F32), axis=0, keepdims=True)
        be_ref[...] = jnp.minimum(owner, float(ne - 1)).astype(I32)
        nu_ref[...] = end_blk[ne - 1:ne, :].astype(I32)
        diag = (lax.broadcasted_iota(I32, (ne, LANES), 0)
                == lax.broadcasted_iota(I32, (ne, LANES), 1))
        ss_ref[...] = jnp.sum(jnp.where(diag, start_blk, 0.0), axis=0, keepdims=True).astype(I32)
        sn_ref[...] = jnp.sum(jnp.where(diag, nblk, 0.0), axis=0, keepdims=True).astype(I32)
        sc_ref[...] = jnp.sum(jnp.where(diag, cnt[...], 0.0), axis=0, keepdims=True).astype(I32)

    @pl.when(p == 1)
    def _():
        before = jnp.dot(onehot.astype(BF16), u_ref[...], preferred_element_type=F32)
        rowpos = base[:, :1] + carry[:, :1] + before
        tok = dest_ref.shape[2]
        for k in range(TOP_K):
            row = jnp.sum(jnp.where(hit[k], rowpos, 0.0), axis=0, keepdims=True).astype(I32)
            for c in range(dest_ref.shape[0]):
                dest_ref[c, k:k + 1, :] = row[:, c * tok:(c + 1) * tok]
        carry[...] += tile_cnt


def _dispatch_plan(eidx, ne, tb, n_blocks, tok_tile):
    n = eidx.shape[1]
    tp = _tile(n, 2 * tok_tile)
    assert n // tb + 1 <= 256
    assert ne <= LANES
    lane_row = pl.BlockSpec((1, LANES), lambda p, t: (0, 0))
    nbp = -(-n_blocks // LANES) * LANES
    upper = (jnp.arange(tp)[:, None] < jnp.arange(tp)[None, :]).astype(BF16)
    lower = (jnp.arange(ne)[None, :] < jnp.arange(ne)[:, None]).astype(BF16)
    return pl.pallas_call(
        functools.partial(_plan_kernel, tb=tb, ne=ne),
        grid=(2, n // tp),
        in_specs=[pl.BlockSpec((TOP_K, tp), lambda p, t: (0, t)),
                  pl.BlockSpec((tp, tp), lambda p, t: (0, 0)),
                  pl.BlockSpec((ne, ne), lambda p, t: (0, 0))],
        out_specs=[pl.BlockSpec((tp // tok_tile, TOP_K, tok_tile), lambda p, t: (t * p, 0, 0)),
                   pl.BlockSpec((1, nbp), lambda p, t: (0, 0)),
                   lane_row, lane_row, lane_row, lane_row],
        out_shape=[jax.ShapeDtypeStruct((n // tok_tile, TOP_K, tok_tile), I32),
                   jax.ShapeDtypeStruct((1, nbp), I32)]
        + [jax.ShapeDtypeStruct((1, LANES), I32)] * 4,
        scratch_shapes=[pltpu.VMEM((ne, LANES), F32)] * 3,
        compiler_params=_cparams("arbitrary", "arbitrary"),
        name="moe_dispatch_plan",
    )(eidx, upper, lower)


def _rows(ref, row, count=1):
    return ref.at[pl.ds(pl.multiple_of(row * ROW_SUB, ROW_SUB), count * ROW_SUB), :]


def _row_copy(src, s_row, dst, d_row, sem):
    return pltpu.make_async_copy(_rows(src, s_row), _rows(dst, d_row), sem)


def _dispatch_kernel(ss_ref, sn_ref, dest_ref, h_ref, sgu_ref, sdn_ref, xs_hbm, sh_ref, zeros, sem,
                     zsem, *, td, tb, ne):
    @pl.when(pl.program_id(0) == 0)
    def _():
        zeros[...] = jnp.zeros_like(zeros)

        def fill_copy(blk):
            return pltpu.make_async_copy(zeros, _rows(xs_hbm, blk * tb, tb), zsem)

        def fill(e, carry):
            @pl.when(sn_ref[e] > 0)
            def _():
                fill_copy(ss_ref[e] + sn_ref[e] - 1).start()
            return carry

        def filled(e, carry):
            @pl.when(sn_ref[e] > 0)
            def _():
                fill_copy(0).wait()
            return carry

        lax.fori_loop(0, ne, fill, 0)
        lax.fori_loop(0, ne, filled, 0)

    def issue(t, carry):
        for k in range(TOP_K):
            _row_copy(h_ref, t, xs_hbm, dest_ref[0, k, t], sem).start(priority=k % DMA_THREADS)
        return carry

    cut1, cut2 = td // 3, 2 * td // 3
    lax.fori_loop(0, cut1, issue, 0)
    ff = sdn_ref.shape[0]
    gu = jnp.dot(_load_row_tiles_bf16(h_ref, 0, td), sgu_ref[...], preferred_element_type=F32)
    act = (_silu(gu[:, :ff]) * gu[:, ff:]).astype(BF16)
    lax.fori_loop(cut1, cut2, issue, 0)
    _store_row_tiles(sh_ref, 0, _pack_rows(jnp.dot(act, sdn_ref[...], preferred_element_type=F32)))
    lax.fori_loop(cut2, td, issue, 0)

    def drain(t, carry):
        for _ in range(TOP_K):
            _row_copy(h_ref, 0, xs_hbm, 0, sem).wait()
        return carry

    lax.fori_loop(0, td, drain, 0)


def _dispatch(dest3, h2p, seg_start, seg_nblk, s_gu, s_dn, n_rows, tb, ne):
    nt, _, td = dest3.shape
    tile = pl.BlockSpec((td * ROW_SUB, LANES), lambda i, ss, sn: (i, 0))
    grid_spec = pltpu.PrefetchScalarGridSpec(
        num_scalar_prefetch=2,
        grid=(nt,),
        in_specs=[pl.BlockSpec((1, TOP_K, td), lambda i, ss, sn: (i, 0, 0),
                               memory_space=pltpu.SMEM),
                  tile,
                  pl.BlockSpec(s_gu.shape, lambda i, ss, sn: (0, 0)),
                  pl.BlockSpec(s_dn.shape, lambda i, ss, sn: (0, 0))],
        out_specs=[pl.BlockSpec(memory_space=pl.ANY), tile],
        scratch_shapes=[pltpu.VMEM((tb * ROW_SUB, LANES), U32),
                        pltpu.SemaphoreType.DMA(()), pltpu.SemaphoreType.DMA(())])
    return pl.pallas_call(
        functools.partial(_dispatch_kernel, td=td, tb=tb, ne=ne),
        grid_spec=grid_spec,
        out_shape=[jax.ShapeDtypeStruct((n_rows * ROW_SUB, LANES), U32),
                   jax.ShapeDtypeStruct(h2p.shape, U32)],
        compiler_params=_cparams("arbitrary"),
        name="moe_row_dispatch",
    )(seg_start, seg_nblk, dest3, h2p, s_gu, s_dn)


def _expert_kernel(be_ref, nu_ref, sn_ref, ss_ref, sc_ref, x_ref, wgu_hbm, wdn_hbm, o_ref, gu_f32,
                   dn_f32, wgu_b, wdn_b, sem, *, layer, ne):
    i = pl.program_id(0)
    live = i < nu_ref[0]
    e = be_ref[i]
    new_expert = (i == 0) | (e != be_ref[jnp.maximum(i - 1, 0)])

    def weight_copies(expert):
        return (pltpu.make_async_copy(wgu_hbm.at[layer, expert], gu_f32, sem.at[0]),
                pltpu.make_async_copy(wdn_hbm.at[layer, expert], dn_f32, sem.at[1]))

    @pl.when(live & (i == 0))
    def _():
        for cp in weight_copies(e):
            cp.start()

    @pl.when(live & new_expert)
    def _():
        for cp in weight_copies(e):
            cp.wait()
        wgu_b[...] = gu_f32[...].astype(BF16)
        wdn_b[...] = dn_f32[...].astype(BF16)
        nxt = lax.while_loop(lambda j: (j < ne) & (sn_ref[jnp.minimum(j, ne - 1)] == 0),
                             lambda j: j + 1, e + 1)

        @pl.when(nxt < ne)
        def _():
            for cp in weight_copies(nxt):
                cp.start()

    tb = x_ref.shape[0] // ROW_SUB
    half = tb // 2
    rows = sc_ref[e] - (i - ss_ref[e]) * tb

    def swiglu_rows(n_rows):
        ff = wdn_b.shape[0]
        x = _load_row_tiles_bf16(x_ref, 0, n_rows)
        gu = jnp.dot(x, wgu_b[...], preferred_element_type=F32)
        act = (_silu(gu[:, :ff]) * gu[:, ff:]).astype(BF16)
        _store_row_tiles(o_ref, 0, _pack_rows(jnp.dot(act, wdn_b[...], preferred_element_type=F32)))

    @pl.when(live & (rows > half))
    def _():
        swiglu_rows(tb)

    @pl.when(live & (rows <= half))
    def _():
        swiglu_rows(half)
        o_ref[half * ROW_SUB:, :] = jnp.zeros((half * ROW_SUB, LANES), U32)


def _expert_ffn(xs, block_e, n_used, seg_nblk, seg_start, seg_cnt, w_gu, w_dn, layer, tb):
    _, ne, d, ff2 = w_gu.shape
    ff = w_dn.shape[2]
    nb = xs.shape[0] // (tb * ROW_SUB)
    row_block = pl.BlockSpec((tb * ROW_SUB, LANES),
                             lambda i, be, nu, sn, ss, sc: (jnp.minimum(i, nu[0] - 1), 0))
    grid_spec = pltpu.PrefetchScalarGridSpec(
        num_scalar_prefetch=5,
        grid=(nb,),
        in_specs=[row_block, pl.BlockSpec(memory_space=pl.ANY), pl.BlockSpec(memory_space=pl.ANY)],
        out_specs=row_block,
        scratch_shapes=[pltpu.VMEM((d, ff2), F32), pltpu.VMEM((ff, d), F32),
                        pltpu.VMEM((d, ff2), BF16), pltpu.VMEM((ff, d), BF16),
                        pltpu.SemaphoreType.DMA((2,))])
    return pl.pallas_call(
        functools.partial(_expert_kernel, layer=layer, ne=ne),
        grid_spec=grid_spec,
        out_shape=jax.ShapeDtypeStruct(xs.shape, U32),
        compiler_params=_cparams("arbitrary"),
        name="moe_expert_ffn",
    )(block_e, n_used, seg_nblk, seg_start, seg_cnt, xs, w_gu, w_dn)


def _combine_kernel(dest_ref, next_ref, wts_ref, ys_hbm, sh_ref, x_ref, mod_ref, fn_ref, o_ref, gbuf,
                    sem, *, tc, final):
    i = pl.program_id(0)
    last = i == pl.num_programs(0) - 1
    slot_rows = TOP_K * tc
    slot = i % 2
    other = 1 - slot

    def start_rows(d_ref, t, to_slot):
        for k in range(TOP_K):
            _row_copy(ys_hbm, d_ref[0, k, t], gbuf, to_slot * slot_rows + k * tc + t,
                      sem.at[to_slot]).start(priority=k % DMA_THREADS)

    def drain(of_slot):
        def body(t, carry):
            for _ in range(TOP_K):
                _row_copy(ys_hbm, 0, gbuf, 0, sem.at[of_slot]).wait()
            return carry

        lax.fori_loop(0, tc, body, 0)

    @pl.when(i == 0)
    def _():
        def body(t, carry):
            start_rows(dest_ref, t, 0)
            return carry

        lax.fori_loop(0, tc, body, 0)

    drain(slot)

    base = slot * slot_rows * ROW_SUB
    gate = mod_ref[0][5:6]

    def group(rg, carry):
        r0 = pl.multiple_of(rg * COMBINE_GROUP, COMBINE_GROUP)
        w = wts_ref[pl.ds(r0, COMBINE_GROUP), :]
        per_k = COMBINE_GROUP // TOP_K
        acc_lo = acc_hi = None
        for k in range(TOP_K):
            for t in range(k * per_k, (k + 1) * per_k):
                start_rows(next_ref, r0 + t, other)
            lo, hi = _load_row_tiles(gbuf, base + (k * tc + r0) * ROW_SUB, COMBINE_GROUP)
            acc_lo = w[:, k:k + 1] * lo if k == 0 else acc_lo + w[:, k:k + 1] * lo
            acc_hi = w[:, k:k + 1] * hi if k == 0 else acc_hi + w[:, k:k + 1] * hi
        lo, hi = _load_row_tiles(sh_ref, r0 * ROW_SUB, COMBINE_GROUP)
        moe = jnp.concatenate([acc_lo + lo, acc_hi + hi], axis=1)
        x2 = x_ref[pl.ds(r0, COMBINE_GROUP), :] + gate * moe
        if final:
            x2 = x2 * lax.rsqrt(jnp.mean(x2 * x2, axis=-1, keepdims=True) + NORM_EPS) * fn_ref[...]
        o_ref[pl.ds(r0, COMBINE_GROUP), :] = x2
        return carry

    lax.fori_loop(0, tc // COMBINE_GROUP, group, 0)

    @pl.when(last)
    def _():
        drain(other)


def _combine(dest3, wts_t, ys, shared_rows, x1, modb, final_norm, seq, final):
    n, d = x1.shape
    nt, _, tc = dest3.shape
    tiles_per_seq = seq // tc
    full2 = lambda i: (0, 0)
    return pl.pallas_call(
        functools.partial(_combine_kernel, tc=tc, final=final),
        grid=(nt,),
        in_specs=[pl.BlockSpec((1, TOP_K, tc), lambda i: (i, 0, 0), memory_space=pltpu.SMEM),
                  pl.BlockSpec((1, TOP_K, tc), lambda i: (jnp.minimum(i + 1, nt - 1), 0, 0),
                               memory_space=pltpu.SMEM),
                  pl.BlockSpec((tc, TOP_K), lambda i: (i, 0)),
                  pl.BlockSpec(memory_space=pl.ANY),
                  pl.BlockSpec((tc * ROW_SUB, LANES), lambda i: (i, 0)),
                  pl.BlockSpec((tc, d), lambda i: (i, 0)),
                  pl.BlockSpec((1, MOD_ROWS, d), lambda i: (i // tiles_per_seq, 0, 0)),
                  pl.BlockSpec((1, d), full2)],
        out_specs=pl.BlockSpec((tc, d), lambda i: (i, 0)),
        out_shape=jax.ShapeDtypeStruct((n, d), F32),
        scratch_shapes=[pltpu.VMEM((2 * TOP_K * tc * ROW_SUB, LANES), U32),
                        pltpu.SemaphoreType.DMA((2,))],
        compiler_params=_cparams("arbitrary"),
        name="moe_combine",
    )(dest3, dest3, wts_t, ys, shared_rows, x1, modb, final_norm.reshape(1, d))


def _moe(x1, h2p, logits_t, modb, bias_perm, w_gu, w_dn, layer, s_gu, s_dn, final_norm, seq,
         final):
    n, d = x1.shape
    ne = w_gu.shape[1]
    tb = EXPERT_BLOCK_ROWS
    tok_tile = _tile(seq, 256)
    n_blocks = n * TOP_K // tb + ne
    eidx, wts = _routing(logits_t, bias_perm)
    dest3, block_e, n_used, seg_start, seg_nblk, seg_cnt = _dispatch_plan(eidx, ne, tb, n_blocks,
                                                                          tok_tile)
    seg_start, seg_nblk, seg_cnt = seg_start[0, :ne], seg_nblk[0, :ne], seg_cnt[0, :ne]
    xs, shared_rows = _dispatch(dest3, h2p, seg_start, seg_nblk, s_gu, s_dn, n_blocks * tb, tb, ne)
    ys = _expert_ffn(xs, block_e[0, :n_blocks], n_used[0, :1], seg_nblk, seg_start, seg_cnt,
                     w_gu, w_dn, layer, tb)
    return _combine(dest3, wts.T, ys, shared_rows, x1, modb, final_norm, seq, final)


def _expert_major_rows(a):
    per = a.shape[0] // N_GROUPS
    return a.reshape((N_GROUPS, per) + a.shape[1:]).swapaxes(0, 1).reshape(a.shape)


def _pairs_to_halves(w, heads):
    d, cols = w.shape
    dk = cols // heads
    return w.reshape(d, heads, dk // 2, 2).transpose(0, 1, 3, 2).reshape(d, cols)


def kernel(x, c, positions, mod_w, mod_b, norm_mix, norm_ffn, ret_w_in, ret_w_out, ret_out_gain, conv_w_in, conv_dw_w, conv_dw_b, conv_ln_g, conv_ln_b, conv_w_out, router_w, router_bias, exp_w_gu, exp_w_down, shared_w_gu, shared_w_down, final_norm):
    batch, seq, d = x.shape
    n = batch * seq
    depth = mod_w.shape[0]
    heads = RET_HEADS
    qk_cols = d
    mods = _modulation(c, mod_w, mod_b)
    xt = x.reshape(n, d)
    for i in range(depth):
        modb = _mod_block(mods[i], d)
        g_mix = norm_mix[i].reshape(1, d)
        g_ffn = norm_ffn[i].reshape(1, d)
        j = i // 2
        if i % 2 == 0:
            w_in = ret_w_in[j]
            w_in = jnp.concatenate([_pairs_to_halves(w_in[:, :qk_cols], heads),
                                    _pairs_to_halves(w_in[:, qk_cols:2 * qk_cols], heads),
                                    w_in[:, 2 * qk_cols:]], axis=1).astype(BF16)
            proj = _normmod_proj(xt, g_mix, modb, w_in, seq, glu=False)
            y = _retention_core(proj, positions, ret_out_gain[j], batch, seq, d)
            w_out = ret_w_out[j].astype(BF16)
        else:
            u = _normmod_proj(xt, g_mix, modb, conv_w_in[j].astype(BF16), seq, glu=True)
            y = _conv_ln_silu(u, conv_dw_w[j], conv_dw_b[j], conv_ln_g[j], conv_ln_b[j], batch, seq)
            w_out = conv_w_out[j].astype(BF16)
        router_wt = _expert_major_rows(router_w[i].T)
        x1, h2p, logits_t = _out_projection(y, w_out, xt, modb, g_ffn, router_wt, seq)
        xt = _moe(x1, h2p, logits_t, modb, _expert_major_rows(router_bias[i]),
                  exp_w_gu, exp_w_down, i,
                  shared_w_gu[i].astype(BF16), shared_w_down[i].astype(BF16),
                  final_norm, seq, final=(i == depth - 1))
    return xt.reshape(batch, seq, d)
```

```python
import functools

import jax
import jax.numpy as jnp
from jax import lax
from jax.experimental import pallas as pl
from jax.experimental.pallas import tpu as pltpu

F32 = jnp.float32
BF16 = jnp.bfloat16
I32 = jnp.int32
U32 = jnp.uint32

RET_HEADS = 8
RET_CHUNK = 256
ROPE_BASE = 10000.0
CONV_WIDTH = 31
CONV_HALO = 32
CONV_ACC_VREGS = 32
N_GROUPS = 8
TOPK_GROUPS = 4
TOP_K = 8
ROUTED_SCALE = 2.5
NORM_EPS = 1e-6
MOD_ROWS = 8
LANES = 128
SUBLANES = 8
ROW_SUB = 8
COMBINE_GROUP = 16
DMA_THREADS = 2
EXPERT_BLOCK_ROWS = 512
VMEM_LIMIT = 56 * 1024 * 1024


def _cparams(*sem):
    return pltpu.CompilerParams(dimension_semantics=sem, vmem_limit_bytes=VMEM_LIMIT)


def _tile(n, pref):
    t = min(n, pref)
    assert n % t == 0, (n, pref)
    return t


def _split_bf16(a):
    hi = a.astype(BF16)
    lo = (a - hi.astype(F32)).astype(BF16)
    return hi, lo


def _dot3(a, b, dims):
    ah, al = _split_bf16(a)
    bh, bl = _split_bf16(b)
    dg = functools.partial(lax.dot_general, dimension_numbers=dims, preferred_element_type=F32)
    return dg(ah, bh) + dg(ah, bl) + dg(al, bh)


_NN = (((1,), (0,)), ((), ()))
_NT = (((1,), (1,)), ((), ()))
_TN = (((0,), (0,)), ((), ()))


def _normmod(x, g, shift, scale):
    y = x * lax.rsqrt(jnp.mean(x * x, axis=-1, keepdims=True) + NORM_EPS)
    return (y * g) * (1.0 + scale) + shift


def _silu(x):
    return x * jax.nn.sigmoid(x)


def _pack_rows(h):
    half = h.shape[1] // 2
    bits = lax.bitcast_convert_type(h.astype(BF16).astype(F32), U32)
    lo = lax.shift_right_logical(bits[:, :half], jnp.uint32(16))
    hi = bits[:, half:] & jnp.uint32(0xFFFF0000)
    return hi | lo


def _unpack_rows(w):
    lo = lax.bitcast_convert_type(lax.shift_left(w, jnp.uint32(16)), F32)
    hi = lax.bitcast_convert_type(w & jnp.uint32(0xFFFF0000), F32)
    return lo, hi


def _store_row_tiles(ref, base, packed):
    t = packed.shape[0]
    assert packed.shape[1] == ROW_SUB * LANES
    for s in range(ROW_SUB):
        ref[pl.ds(base + s, t, stride=ROW_SUB), :] = packed[:, s * LANES:(s + 1) * LANES]


def _load_row_tiles(ref, base, t):
    parts = [_unpack_rows(ref[pl.ds(base + s, t, stride=ROW_SUB), :]) for s in range(ROW_SUB)]
    lo = jnp.concatenate([p[0] for p in parts], axis=1)
    hi = jnp.concatenate([p[1] for p in parts], axis=1)
    return lo, hi


def _load_row_tiles_bf16(ref, base, t):
    lo, hi = _load_row_tiles(ref, base, t)
    return jnp.concatenate([lo.astype(BF16), hi.astype(BF16)], axis=1)


def _mod_kernel(c_ref, w_ref, b_ref, o_ref):
    c = c_ref[...]
    o_ref[0] = _dot3(_silu(c), w_ref[0], _NN) + b_ref[0]


def _modulation(c, mod_w, mod_b):
    depth, d, n6 = mod_w.shape
    b = c.shape[0]
    assert b <= MOD_ROWS
    c_pad = jnp.zeros((MOD_ROWS, d), F32).at[:b].set(c)
    tn = _tile(n6, 1024)
    out = pl.pallas_call(
        _mod_kernel,
        grid=(depth, n6 // tn),
        in_specs=[pl.BlockSpec((MOD_ROWS, d), lambda i, j: (0, 0)),
                  pl.BlockSpec((1, d, tn), lambda i, j: (i, 0, j)),
                  pl.BlockSpec((1, 1, tn), lambda i, j: (i, 0, j))],
        out_specs=pl.BlockSpec((1, MOD_ROWS, tn), lambda i, j: (i, 0, j)),
        out_shape=jax.ShapeDtypeStruct((depth, MOD_ROWS, n6), F32),
        compiler_params=_cparams("arbitrary", "arbitrary"),
        name="adaln_modulation",
    )(c_pad, mod_w, mod_b.reshape(depth, 1, n6))
    return out[:, :b]


def _mod_block(mod_i, d):
    b = mod_i.shape[0]
    m = mod_i.reshape(b, 6, d)
    return jnp.concatenate([m, jnp.zeros((b, MOD_ROWS - 6, d), F32)], axis=1)


def _proj_kernel(x_ref, g_ref, mod_ref, w_ref, o_ref, h_scr):
    @pl.when(pl.program_id(1) == 0)
    def _():
        m = mod_ref[0]
        h_scr[...] = _normmod(x_ref[...], g_ref[...], m[0:1], m[1:2]).astype(BF16)

    o_ref[...] = jnp.dot(h_scr[...], w_ref[...], preferred_element_type=F32).astype(o_ref.dtype)


def _glu_proj_kernel(x_ref, g_ref, mod_ref, wa_ref, wb_ref, o_ref, h_scr):
    @pl.when(pl.program_id(1) == 0)
    def _():
        m = mod_ref[0]
        h_scr[...] = _normmod(x_ref[...], g_ref[...], m[0:1], m[1:2]).astype(BF16)

    h = h_scr[...]
    a = jnp.dot(h, wa_ref[...], preferred_element_type=F32)
    b = jnp.dot(h, wb_ref[...], preferred_element_type=F32)
    o_ref[...] = a * jax.nn.sigmoid(b)


def _normmod_proj(x, g, modb, w, seq, glu):
    n, d = x.shape
    nout = w.shape[1] // 2 if glu else w.shape[1]
    tm = _tile(seq, 1024)
    tn = _tile(nout, 512 if glu else 1024)
    tiles_per_seq = seq // tm
    x_spec = pl.BlockSpec((tm, d), lambda i, j: (i, 0))
    g_spec = pl.BlockSpec((1, d), lambda i, j: (0, 0))
    m_spec = pl.BlockSpec((1, MOD_ROWS, d), lambda i, j: (i // tiles_per_seq, 0, 0))
    if glu:
        half_blocks = nout // tn
        in_specs = [x_spec, g_spec, m_spec,
                    pl.BlockSpec((d, tn), lambda i, j: (0, j)),
                    pl.BlockSpec((d, tn), lambda i, j: (0, j + half_blocks))]
        body, args, odt = _glu_proj_kernel, (x, g, modb, w, w), F32
    else:
        in_specs = [x_spec, g_spec, m_spec, pl.BlockSpec((d, tn), lambda i, j: (0, j))]
        body, args, odt = _proj_kernel, (x, g, modb, w), BF16
    return pl.pallas_call(
        body,
        grid=(n // tm, nout // tn),
        in_specs=in_specs,
        out_specs=pl.BlockSpec((tm, tn), lambda i, j: (i, j)),
        out_shape=jax.ShapeDtypeStruct((n, nout), odt),
        scratch_shapes=[pltpu.VMEM((tm, d), BF16)],
        compiler_params=_cparams("arbitrary", "arbitrary"),
        name="glu_in_projection" if glu else "in_projection",
    )(*args)


def _retention_kernel(pos_ref, inv_ref, q_ref, k_ref, v_ref, g_ref, intra_ref, qd_ref, kd_ref,
                      cd_ref, gain_ref, o_ref, state, *, dk, dv):
    @pl.when(pl.program_id(1) == 0)
    def _():
        state[...] = jnp.zeros_like(state)

    half = dk // 2
    ang = pos_ref[...].astype(F32) * inv_ref[...]
    cos = jnp.cos(ang)
    sin = jnp.sin(ang)

    def rot(ref, h):
        x1 = ref[:, h * dk:h * dk + half].astype(F32)
        x2 = ref[:, h * dk + half:(h + 1) * dk].astype(F32)
        return jnp.concatenate([x1 * cos - x2 * sin, x1 * sin + x2 * cos], axis=1)

    for h in range(RET_HEADS):
        q = rot(q_ref, h)
        k = rot(k_ref, h) * (dk ** -0.5)
        v = v_ref[:, h * dv:(h + 1) * dv]
        s = lax.dot_general(q.astype(BF16), k.astype(BF16), _NT, preferred_element_type=F32)
        p = (s * intra_ref[h]).astype(BF16)
        inner = jnp.dot(p, v, preferred_element_type=F32)
        st = state[h]
        cross = jnp.dot((q * qd_ref[h]).astype(BF16), st.astype(BF16),
                        preferred_element_type=F32)
        kv = lax.dot_general((k * kd_ref[h]).astype(BF16), v, _TN, preferred_element_type=F32)
        state[h] = cd_ref[h][:, :1] * st + kv
        o = inner + cross
        o = o * lax.rsqrt(jnp.mean(o * o, axis=-1, keepdims=True) + NORM_EPS)
        gate = g_ref[:, h * dv:(h + 1) * dv].astype(F32)
        o_ref[:, h * dv:(h + 1) * dv] = (_silu(gate) * (o * gain_ref[:, h * dv:(h + 1) * dv])
                                         ).astype(o_ref.dtype)


def _retention_core(proj, positions, out_gain, batch, seq, d):
    n = batch * seq
    heads, dk = RET_HEADS, d // RET_HEADS
    dv = 2 * dk
    c = _tile(seq, RET_CHUNK)
    nc = seq // c
    half = dk // 2
    inv = (1.0 / (ROPE_BASE ** jnp.linspace(0.0, 1.0, half, dtype=F32))).reshape(1, half)
    log_gamma = jnp.log1p(-jnp.exp2(-5.0 - jnp.arange(heads, dtype=F32)))
    idx = jnp.arange(c, dtype=F32)
    rel = idx[:, None] - idx[None, :]
    intra = jnp.where(rel >= 0, jnp.exp(log_gamma[:, None, None] * jnp.maximum(rel, 0.0)), 0.0)
    q_decay = jnp.exp(log_gamma[:, None] * (idx + 1.0))[:, :, None]
    k_decay = jnp.exp(log_gamma[:, None] * (c - 1.0 - idx))[:, :, None]
    chunk_decay = jnp.broadcast_to(jnp.exp(log_gamma * c)[:, None, None], (heads, 1, LANES))
    row = lambda b, j: b * nc + j
    full3 = lambda b, j: (0, 0, 0)
    return pl.pallas_call(
        functools.partial(_retention_kernel, dk=dk, dv=dv),
        grid=(batch, nc),
        in_specs=[pl.BlockSpec((c, 1), lambda b, j: (row(b, j), 0)),
                  pl.BlockSpec((1, half), lambda b, j: (0, 0)),
                  pl.BlockSpec((c, heads * dk), lambda b, j: (row(b, j), 0)),
                  pl.BlockSpec((c, heads * dk), lambda b, j: (row(b, j), 1)),
                  pl.BlockSpec((c, heads * dv), lambda b, j: (row(b, j), 1)),
                  pl.BlockSpec((c, heads * dv), lambda b, j: (row(b, j), 2)),
                  pl.BlockSpec((heads, c, c), full3),
                  pl.BlockSpec((heads, c, 1), full3),
                  pl.BlockSpec((heads, c, 1), full3),
                  pl.BlockSpec((heads, 1, LANES), full3),
                  pl.BlockSpec((1, heads * dv), lambda b, j: (0, 0))],
        out_specs=pl.BlockSpec((c, heads * dv), lambda b, j: (row(b, j), 0)),
        out_shape=jax.ShapeDtypeStruct((n, heads * dv), BF16),
        scratch_shapes=[pltpu.VMEM((heads, dk, dv), F32)],
        compiler_params=_cparams("arbitrary", "arbitrary"),
        name="retention_core",
    )(positions.reshape(n, 1), inv, proj, proj, proj, proj, intra, q_decay, k_decay,
      chunk_decay, out_gain.reshape(1, heads * dv))


def _conv_kernel(u_ref, halo_ref, w_ref, b_ref, lg_ref, lb_ref, o_ref, ext, conv, *, tile):
    first = pl.program_id(1) == 0
    ext[0, 0:CONV_HALO, :] = jnp.where(first, 0.0, halo_ref[...])
    ext[0, CONV_HALO:CONV_HALO + tile, :] = u_ref[...]
    span = ext.shape[1] - SUBLANES
    for s in range(1, SUBLANES):
        ext[s, 0:span, :] = ext[0, s:s + span, :]
    base = CONV_HALO - (CONV_WIDTH - 1)
    ch = u_ref.shape[1]
    cw = min(ch, max(LANES, CONV_ACC_VREGS * SUBLANES // tile * LANES))
    for c0 in range(0, ch, cw):
        part = jnp.zeros((tile, cw), F32) + b_ref[:, c0:c0 + cw]
        for j in range(CONV_WIDTH):
            shift, start = (base + j) % SUBLANES, (base + j) // SUBLANES * SUBLANES
            part = part + ext[shift, start:start + tile, c0:c0 + cw] * w_ref[j:j + 1, c0:c0 + cw]
        conv[:, c0:c0 + cw] = part
    acc = conv[...]
    mu = jnp.mean(acc, axis=-1, keepdims=True)
    cen = acc - mu
    var = jnp.mean(cen * cen, axis=-1, keepdims=True)
    y = cen * lax.rsqrt(var + NORM_EPS) * lg_ref[...] + lb_ref[...]
    o_ref[...] = _silu(y).astype(o_ref.dtype)


def _conv_ln_silu(u, dw_w, dw_b, ln_g, ln_b, batch, seq):
    n, ch = u.shape
    t = _tile(seq, 128)
    assert t % CONV_HALO == 0
    nt = seq // t
    halo_per_tile = t // CONV_HALO
    row = lambda b, j: b * nt + j
    vec = pl.BlockSpec((1, ch), lambda b, j: (0, 0))
    return pl.pallas_call(
        functools.partial(_conv_kernel, tile=t),
        grid=(batch, nt),
        in_specs=[pl.BlockSpec((t, ch), lambda b, j: (row(b, j), 0)),
                  pl.BlockSpec((CONV_HALO, ch),
                               lambda b, j: (jnp.maximum(row(b, j) * halo_per_tile - 1, 0), 0)),
                  pl.BlockSpec((CONV_WIDTH, ch), lambda b, j: (0, 0)),
                  vec, vec, vec],
        out_specs=pl.BlockSpec((t, ch), lambda b, j: (row(b, j), 0)),
        out_shape=jax.ShapeDtypeStruct((n, ch), BF16),
        scratch_shapes=[pltpu.VMEM((SUBLANES, CONV_HALO + t, ch), F32),
                        pltpu.VMEM((t, ch), F32)],
        compiler_params=_cparams("arbitrary", "arbitrary"),
        name="conv_ln_silu",
    )(u, u, dw_w, dw_b.reshape(1, ch), ln_g.reshape(1, ch), ln_b.reshape(1, ch))


def _outproj_kernel(y_ref, w_ref, x_ref, mod_ref, g_ref, rw_ref, x1_ref, h2_ref, lg_ref, acc):
    k = pl.program_id(1)

    @pl.when(k == 0)
    def _():
        acc[...] = jnp.zeros_like(acc)

    acc[...] += jnp.dot(y_ref[...], w_ref[...], preferred_element_type=F32)

    @pl.when(k == pl.num_programs(1) - 1)
    def _():
        m = mod_ref[0]
        x1 = x_ref[...] + m[2:3] * acc[...]
        x1_ref[...] = x1
        h2 = _normmod(x1, g_ref[...], m[3:4], m[4:5])
        _store_row_tiles(h2_ref, 0, _pack_rows(h2))
        lg_ref[...] = _dot3(rw_ref[...], h2, _NT)


def _out_projection(y, w, x, modb, g_ffn, router_wt, seq):
    n, kdim = y.shape
    d = x.shape[1]
    ne = router_wt.shape[0]
    assert d // 2 == ROW_SUB * LANES
    tm = _tile(seq, 512)
    tk = _tile(kdim, 1024)
    tiles_per_seq = seq // tm
    return pl.pallas_call(
        _outproj_kernel,
        grid=(n // tm, kdim // tk),
        in_specs=[pl.BlockSpec((tm, tk), lambda i, k: (i, k)),
                  pl.BlockSpec((tk, d), lambda i, k: (k, 0)),
                  pl.BlockSpec((tm, d), lambda i, k: (i, 0)),
                  pl.BlockSpec((1, MOD_ROWS, d), lambda i, k: (i // tiles_per_seq, 0, 0)),
                  pl.BlockSpec((1, d), lambda i, k: (0, 0)),
                  pl.BlockSpec((ne, d), lambda i, k: (0, 0))],
        out_specs=[pl.BlockSpec((tm, d), lambda i, k: (i, 0)),
                   pl.BlockSpec((tm * ROW_SUB, LANES), lambda i, k: (i, 0)),
                   pl.BlockSpec((ne, tm), lambda i, k: (0, i))],
        out_shape=[jax.ShapeDtypeStruct((n, d), F32),
                   jax.ShapeDtypeStruct((n * ROW_SUB, LANES), U32),
                   jax.ShapeDtypeStruct((ne, n), F32)],
        scratch_shapes=[pltpu.VMEM((tm, d), F32)],
        compiler_params=_cparams("arbitrary", "arbitrary"),
        name="out_projection",
    )(y, w, x, modb, g_ffn, router_wt)


def _routing_kernel(lg_ref, bias_ref, eidx_ref, wts_ref):
    neg = -jnp.inf
    per = N_GROUPS
    tr = lg_ref.shape[1]
    scores = jax.nn.sigmoid(lg_ref[...])
    biased = scores + bias_ref[...]
    n_slab = lg_ref.shape[0] // per
    s_j = [scores[per * j:per * (j + 1), :] for j in range(n_slab)]
    b_j = [biased[per * j:per * (j + 1), :] for j in range(n_slab)]

    m1 = functools.reduce(jnp.maximum, b_j)
    ties = functools.reduce(lambda a, b: a + b, [(b == m1).astype(F32) for b in b_j])
    below = functools.reduce(jnp.maximum, [jnp.where(b < m1, b, neg) for b in b_j])
    grp = m1 + jnp.where(ties >= 2.0, m1, below)

    gid = lax.broadcasted_iota(I32, (per, tr), 0).astype(F32)
    chosen = jnp.zeros((per, tr), F32)
    cur = grp
    for _ in range(TOPK_GROUPS):
        mx = jnp.max(cur, axis=0, keepdims=True)
        first = jnp.min(jnp.where(cur == mx, gid, float(per)), axis=0, keepdims=True)
        pick = gid == first
        chosen = jnp.where(pick, 1.0, chosen)
        cur = jnp.where(pick, neg, cur)

    eid_j = [gid * float(n_slab) + float(j) for j in range(n_slab)]
    cur_j = [jnp.where(chosen > 0.0, b, neg) for b in b_j]
    picked_scores = []
    for k in range(TOP_K):
        mx = jnp.max(functools.reduce(jnp.maximum, cur_j), axis=0, keepdims=True)
        cand = functools.reduce(
            jnp.minimum, [jnp.where(c == mx, e, float(per * n_slab)) for c, e in zip(cur_j, eid_j)])
        first = jnp.min(cand, axis=0, keepdims=True)
        pick_j = [e == first for e in eid_j]
        sc = functools.reduce(lambda a, b: a + b,
                              [jnp.where(p, s, 0.0) for p, s in zip(pick_j, s_j)])
        picked_scores.append(jnp.sum(sc, axis=0, keepdims=True))
        cur_j = [jnp.where(p, neg, c) for p, c in zip(pick_j, cur_j)]
        eidx_ref[k:k + 1, :] = first.astype(I32)
    total = functools.reduce(lambda a, b: a + b, picked_scores)
    for k in range(TOP_K):
        wts_ref[k:k + 1, :] = picked_scores[k] / total * ROUTED_SCALE


def _routing(logits_t, bias_perm):
    ne, n = logits_t.shape
    tr = _tile(n, 1024)
    return pl.pallas_call(
        _routing_kernel,
        grid=(n // tr,),
        in_specs=[pl.BlockSpec((ne, tr), lambda i: (0, i)),
                  pl.BlockSpec((ne, 1), lambda i: (0, 0))],
        out_specs=[pl.BlockSpec((TOP_K, tr), lambda i: (0, i)),
                   pl.BlockSpec((TOP_K, tr), lambda i: (0, i))],
        out_shape=[jax.ShapeDtypeStruct((TOP_K, n), I32),
                   jax.ShapeDtypeStruct((TOP_K, n), F32)],
        compiler_params=_cparams("arbitrary"),
        name="moe_routing",
    )(logits_t, bias_perm.reshape(ne, 1))


def _plan_kernel(e_ref, u_ref, l_ref, dest_ref, be_ref, nu_ref, ss_ref, sn_ref, sc_ref, cnt, base,
                 carry, *, tb, ne):
    p = pl.program_id(0)
    t = pl.program_id(1)
    tp = e_ref.shape[1]
    ei = e_ref[...]
    eid = lax.broadcasted_iota(I32, (ne, tp), 0)
    hit = [ei[k:k + 1, :] == eid for k in range(TOP_K)]
    onehot = functools.reduce(lambda a, b: a + b, [h.astype(F32) for h in hit])
    tile_cnt = jnp.sum(onehot, axis=1, keepdims=True)

    @pl.when((p == 0) & (t == 0))
    def _():
        cnt[...] = jnp.zeros_like(cnt)

    @pl.when(p == 0)
    def _():
        cnt[...] += tile_cnt

    @pl.when((p == 1) & (t == 0))
    def _():
        nblk = jnp.floor((cnt[...] + float(tb - 1)) * (1.0 / tb))
        start_blk = jnp.dot(l_ref[...], nblk.astype(BF16), preferred_element_type=F32)
        base[...] = start_blk * float(tb)
        carry[...] = jnp.zeros_like(carry)
        end_blk = start_blk + nblk
        nbp = be_ref.shape[1]
        blk = lax.broadcasted_iota(I32, (ne, nbp), 1).astype(F32)
        owner = jnp.sum((end_blk[:, :1] <= blk).astype(F32), axis=0, keepdims=True)
        be_ref[...] = jnp.minimum(owner, float(ne - 1)).astype(I32)
        nu_ref[...] = end_blk[ne - 1:ne, :].astype(I32)
        diag = (lax.broadcasted_iota(I32, (ne, LANES), 0)
                == lax.broadcasted_iota(I32, (ne, LANES), 1))
        ss_ref[...] = jnp.sum(jnp.where(diag, start_blk, 0.0), axis=0, keepdims=True).astype(I32)
        sn_ref[...] = jnp.sum(jnp.where(diag, nblk, 0.0), axis=0, keepdims=True).astype(I32)
        sc_ref[...] = jnp.sum(jnp.where(diag, cnt[...], 0.0), axis=0, keepdims=True).astype(I32)

    @pl.when(p == 1)
    def _():
        before = jnp.dot(onehot.astype(BF16), u_ref[...], preferred_element_type=F32)
        rowpos = base[:, :1] + carry[:, :1] + before
        tok = dest_ref.shape[2]
        for k in range(TOP_K):
            row = jnp.sum(jnp.where(hit[k], rowpos, 0.0), axis=0, keepdims=True).astype(I32)
            for c in range(dest_ref.shape[0]):
                dest_ref[c, k:k + 1, :] = row[:, c * tok:(c + 1) * tok]
        carry[...] += tile_cnt


def _dispatch_plan(eidx, ne, tb, n_blocks, tok_tile):
    n = eidx.shape[1]
    tp = _tile(n, 2 * tok_tile)
    assert n // tb + 1 <= 256
    assert ne <= LANES
    lane_row = pl.BlockSpec((1, LANES), lambda p, t: (0, 0))
    nbp = -(-n_blocks // LANES) * LANES
    upper = (jnp.arange(tp)[:, None] < jnp.arange(tp)[None, :]).astype(BF16)
    lower = (jnp.arange(ne)[None, :] < jnp.arange(ne)[:, None]).astype(BF16)
    return pl.pallas_call(
        functools.partial(_plan_kernel, tb=tb, ne=ne),
        grid=(2, n // tp),
        in_specs=[pl.BlockSpec((TOP_K, tp), lambda p, t: (0, t)),
                  pl.BlockSpec((tp, tp), lambda p, t: (0, 0)),
                  pl.BlockSpec((ne, ne), lambda p, t: (0, 0))],
        out_specs=[pl.BlockSpec((tp // tok_tile, TOP_K, tok_tile), lambda p, t: (t * p, 0, 0)),
                   pl.BlockSpec((1, nbp), lambda p, t: (0, 0)),
                   lane_row, lane_row, lane_row, lane_row],
        out_shape=[jax.ShapeDtypeStruct((n // tok_tile, TOP_K, tok_tile), I32),
                   jax.ShapeDtypeStruct((1, nbp), I32)]
        + [jax.ShapeDtypeStruct((1, LANES), I32)] * 4,
        scratch_shapes=[pltpu.VMEM((ne, LANES), F32)] * 3,
        compiler_params=_cparams("arbitrary", "arbitrary"),
        name="moe_dispatch_plan",
    )(eidx, upper, lower)


def _rows(ref, row, count=1):
    return ref.at[pl.ds(pl.multiple_of(row * ROW_SUB, ROW_SUB), count * ROW_SUB), :]


def _row_copy(src, s_row, dst, d_row, sem):
    return pltpu.make_async_copy(_rows(src, s_row), _rows(dst, d_row), sem)


def _dispatch_kernel(ss_ref, sn_ref, dest_ref, h_ref, sgu_ref, sdn_ref, xs_hbm, sh_ref, zeros, sem,
                     zsem, *, td, tb, ne):
    @pl.when(pl.program_id(0) == 0)
    def _():
        zeros[...] = jnp.zeros_like(zeros)

        def fill_copy(blk):
            return pltpu.make_async_copy(zeros, _rows(xs_hbm, blk * tb, tb), zsem)

        def fill(e, carry):
            @pl.when(sn_ref[e] > 0)
            def _():
                fill_copy(ss_ref[e] + sn_ref[e] - 1).start()
            return carry

        def filled(e, carry):
            @pl.when(sn_ref[e] > 0)
            def _():
                fill_copy(0).wait()
            return carry

        lax.fori_loop(0, ne, fill, 0)
        lax.fori_loop(0, ne, filled, 0)

    def issue(t, carry):
        for k in range(TOP_K):
            _row_copy(h_ref, t, xs_hbm, dest_ref[0, k, t], sem).start(priority=k % DMA_THREADS)
        return carry

    cut1, cut2 = td // 3, 2 * td // 3
    lax.fori_loop(0, cut1, issue, 0)
    ff = sdn_ref.shape[0]
    gu = jnp.dot(_load_row_tiles_bf16(h_ref, 0, td), sgu_ref[...], preferred_element_type=F32)
    act = (_silu(gu[:, :ff]) * gu[:, ff:]).astype(BF16)
    lax.fori_loop(cut1, cut2, issue, 0)
    _store_row_tiles(sh_ref, 0, _pack_rows(jnp.dot(act, sdn_ref[...], preferred_element_type=F32)))
    lax.fori_loop(cut2, td, issue, 0)

    def drain(t, carry):
        for _ in range(TOP_K):
            _row_copy(h_ref, 0, xs_hbm, 0, sem).wait()
        return carry

    lax.fori_loop(0, td, drain, 0)


def _dispatch(dest3, h2p, seg_start, seg_nblk, s_gu, s_dn, n_rows, tb, ne):
    nt, _, td = dest3.shape
    tile = pl.BlockSpec((td * ROW_SUB, LANES), lambda i, ss, sn: (i, 0))
    grid_spec = pltpu.PrefetchScalarGridSpec(
        num_scalar_prefetch=2,
        grid=(nt,),
        in_specs=[pl.BlockSpec((1, TOP_K, td), lambda i, ss, sn: (i, 0, 0),
                               memory_space=pltpu.SMEM),
                  tile,
                  pl.BlockSpec(s_gu.shape, lambda i, ss, sn: (0, 0)),
                  pl.BlockSpec(s_dn.shape, lambda i, ss, sn: (0, 0))],
        out_specs=[pl.BlockSpec(memory_space=pl.ANY), tile],
        scratch_shapes=[pltpu.VMEM((tb * ROW_SUB, LANES), U32),
                        pltpu.SemaphoreType.DMA(()), pltpu.SemaphoreType.DMA(())])
    return pl.pallas_call(
        functools.partial(_dispatch_kernel, td=td, tb=tb, ne=ne),
        grid_spec=grid_spec,
        out_shape=[jax.ShapeDtypeStruct((n_rows * ROW_SUB, LANES), U32),
                   jax.ShapeDtypeStruct(h2p.shape, U32)],
        compiler_params=_cparams("arbitrary"),
        name="moe_row_dispatch",
    )(seg_start, seg_nblk, dest3, h2p, s_gu, s_dn)


def _expert_kernel(be_ref, nu_ref, sn_ref, ss_ref, sc_ref, x_ref, wgu_hbm, wdn_hbm, o_ref, gu_f32,
                   dn_f32, wgu_b, wdn_b, sem, *, layer, ne):
    i = pl.program_id(0)
    live = i < nu_ref[0]
    e = be_ref[i]
    new_expert = (i == 0) | (e != be_ref[jnp.maximum(i - 1, 0)])

    def weight_copies(expert):
        return (pltpu.make_async_copy(wgu_hbm.at[layer, expert], gu_f32, sem.at[0]),
                pltpu.make_async_copy(wdn_hbm.at[layer, expert], dn_f32, sem.at[1]))

    @pl.when(live & (i == 0))
    def _():
        for cp in weight_copies(e):
            cp.start()

    @pl.when(live & new_expert)
    def _():
        for cp in weight_copies(e):
            cp.wait()
        wgu_b[...] = gu_f32[...].astype(BF16)
        wdn_b[...] = dn_f32[...].astype(BF16)
        nxt = lax.while_loop(lambda j: (j < ne) & (sn_ref[jnp.minimum(j, ne - 1)] == 0),
                             lambda j: j + 1, e + 1)

        @pl.when(nxt < ne)
        def _():
            for cp in weight_copies(nxt):
                cp.start()

    tb = x_ref.shape[0] // ROW_SUB
    half = tb // 2
    rows = sc_ref[e] - (i - ss_ref[e]) * tb

    def swiglu_rows(n_rows):
        ff = wdn_b.shape[0]
        x = _load_row_tiles_bf16(x_ref, 0, n_rows)
        gu = jnp.dot(x, wgu_b[...], preferred_element_type=F32)
        act = (_silu(gu[:, :ff]) * gu[:, ff:]).astype(BF16)
        _store_row_tiles(o_ref, 0, _pack_rows(jnp.dot(act, wdn_b[...], preferred_element_type=F32)))

    @pl.when(live & (rows > half))
    def _():
        swiglu_rows(tb)

    @pl.when(live & (rows <= half))
    def _():
        swiglu_rows(half)
        o_ref[half * ROW_SUB:, :] = jnp.zeros((half * ROW_SUB, LANES), U32)


def _expert_ffn(xs, block_e, n_used, seg_nblk, seg_start, seg_cnt, w_gu, w_dn, layer, tb):
    _, ne, d, ff2 = w_gu.shape
    ff = w_dn.shape[2]
    nb = xs.shape[0] // (tb * ROW_SUB)
    row_block = pl.BlockSpec((tb * ROW_SUB, LANES),
                             lambda i, be, nu, sn, ss, sc: (jnp.minimum(i, nu[0] - 1), 0))
    grid_spec = pltpu.PrefetchScalarGridSpec(
        num_scalar_prefetch=5,
        grid=(nb,),
        in_specs=[row_block, pl.BlockSpec(memory_space=pl.ANY), pl.BlockSpec(memory_space=pl.ANY)],
        out_specs=row_block,
        scratch_shapes=[pltpu.VMEM((d, ff2), F32), pltpu.VMEM((ff, d), F32),
                        pltpu.VMEM((d, ff2), BF16), pltpu.VMEM((ff, d), BF16),
                        pltpu.SemaphoreType.DMA((2,))])
    return pl.pallas_call(
        functools.partial(_expert_kernel, layer=layer, ne=ne),
        grid_spec=grid_spec,
        out_shape=jax.ShapeDtypeStruct(xs.shape, U32),
        compiler_params=_cparams("arbitrary"),
        name="moe_expert_ffn",
    )(block_e, n_used, seg_nblk, seg_start, seg_cnt, xs, w_gu, w_dn)


def _combine_kernel(dest_ref, next_ref, wts_ref, ys_hbm, sh_ref, x_ref, mod_ref, fn_ref, o_ref, gbuf,
                    sem, *, tc, final):
    i = pl.program_id(0)
    last = i == pl.num_programs(0) - 1
    slot_rows = TOP_K * tc
    slot = i % 2
    other = 1 - slot

    def start_rows(d_ref, t, to_slot):
        for k in range(TOP_K):
            _row_copy(ys_hbm, d_ref[0, k, t], gbuf, to_slot * slot_rows + k * tc + t,
                      sem.at[to_slot]).start(priority=k % DMA_THREADS)

    def drain(of_slot):
        def body(t, carry):
            for _ in range(TOP_K):
                _row_copy(ys_hbm, 0, gbuf, 0, sem.at[of_slot]).wait()
            return carry

        lax.fori_loop(0, tc, body, 0)

    @pl.when(i == 0)
    def _():
        def body(t, carry):
            start_rows(dest_ref, t, 0)
            return carry

        lax.fori_loop(0, tc, body, 0)

    drain(slot)

    base = slot * slot_rows * ROW_SUB
    gate = mod_ref[0][5:6]

    def group(rg, carry):
        r0 = pl.multiple_of(rg * COMBINE_GROUP, COMBINE_GROUP)
        w = wts_ref[pl.ds(r0, COMBINE_GROUP), :]
        per_k = COMBINE_GROUP // TOP_K
        acc_lo = acc_hi = None
        for k in range(TOP_K):
            for t in range(k * per_k, (k + 1) * per_k):
                start_rows(next_ref, r0 + t, other)
            lo, hi = _load_row_tiles(gbuf, base + (k * tc + r0) * ROW_SUB, COMBINE_GROUP)
            acc_lo = w[:, k:k + 1] * lo if k == 0 else acc_lo + w[:, k:k + 1] * lo
            acc_hi = w[:, k:k + 1] * hi if k == 0 else acc_hi + w[:, k:k + 1] * hi
        lo, hi = _load_row_tiles(sh_ref, r0 * ROW_SUB, COMBINE_GROUP)
        moe = jnp.concatenate([acc_lo + lo, acc_hi + hi], axis=1)
        o_ref[pl.ds(r0, COMBINE_GROUP), :] = x_ref[pl.ds(r0, COMBINE_GROUP), :] + gate * moe
        return carry

    lax.fori_loop(0, tc // COMBINE_GROUP, group, 0)
    if final:
        x2 = o_ref[...]
        o_ref[...] = (x2 * lax.rsqrt(jnp.mean(x2 * x2, axis=-1, keepdims=True) + NORM_EPS)
                      * fn_ref[...])

    @pl.when(last)
    def _():
        drain(other)


def _combine(dest3, wts_t, ys, shared_rows, x1, modb, final_norm, seq, final):
    n, d = x1.shape
    nt, _, tc = dest3.shape
    tiles_per_seq = seq // tc
    full2 = lambda i: (0, 0)
    return pl.pallas_call(
        functools.partial(_combine_kernel, tc=tc, final=final),
        grid=(nt,),
        in_specs=[pl.BlockSpec((1, TOP_K, tc), lambda i: (i, 0, 0), memory_space=pltpu.SMEM),
                  pl.BlockSpec((1, TOP_K, tc), lambda i: (jnp.minimum(i + 1, nt - 1), 0, 0),
                               memory_space=pltpu.SMEM),
                  pl.BlockSpec((tc, TOP_K), lambda i: (i, 0)),
                  pl.BlockSpec(memory_space=pl.ANY),
                  pl.BlockSpec((tc * ROW_SUB, LANES), lambda i: (i, 0)),
                  pl.BlockSpec((tc, d), lambda i: (i, 0)),
                  pl.BlockSpec((1, MOD_ROWS, d), lambda i: (i // tiles_per_seq, 0, 0)),
                  pl.BlockSpec((1, d), full2)],
        out_specs=pl.BlockSpec((tc, d), lambda i: (i, 0)),
        out_shape=jax.ShapeDtypeStruct((n, d), F32),
        scratch_shapes=[pltpu.VMEM((2 * TOP_K * tc * ROW_SUB, LANES), U32),
                        pltpu.SemaphoreType.DMA((2,))],
        compiler_params=_cparams("arbitrary"),
        name="moe_combine",
    )(dest3, dest3, wts_t, ys, shared_rows, x1, modb, final_norm.reshape(1, d))


def _moe(x1, h2p, logits_t, modb, bias_perm, w_gu, w_dn, layer, s_gu, s_dn, final_norm, seq,
         final):
    n, d = x1.shape
    ne = w_gu.shape[1]
    tb = EXPERT_BLOCK_ROWS
    tok_tile = _tile(seq, 256)
    n_blocks = n * TOP_K // tb + ne
    eidx, wts = _routing(logits_t, bias_perm)
    dest3, block_e, n_used, seg_start, seg_nblk, seg_cnt = _dispatch_plan(eidx, ne, tb, n_blocks,
                                                                          tok_tile)
    seg_start, seg_nblk, seg_cnt = seg_start[0, :ne], seg_nblk[0, :ne], seg_cnt[0, :ne]
    xs, shared_rows = _dispatch(dest3, h2p, seg_start, seg_nblk, s_gu, s_dn, n_blocks * tb, tb, ne)
    ys = _expert_ffn(xs, block_e[0, :n_blocks], n_used[0, :1], seg_nblk, seg_start, seg_cnt,
                     w_gu, w_dn, layer, tb)
    return _combine(dest3, wts.T, ys, shared_rows, x1, modb, final_norm, seq, final)


def _expert_major_rows(a):
    per = a.shape[0] // N_GROUPS
    return a.reshape((N_GROUPS, per) + a.shape[1:]).swapaxes(0, 1).reshape(a.shape)


def _pairs_to_halves(w, heads):
    d, cols = w.shape
    dk = cols // heads
    return w.reshape(d, heads, dk // 2, 2).transpose(0, 1, 3, 2).reshape(d, cols)


def kernel(x, c, positions, mod_w, mod_b, norm_mix, norm_ffn, ret_w_in, ret_w_out, ret_out_gain, conv_w_in, conv_dw_w, conv_dw_b, conv_ln_g, conv_ln_b, conv_w_out, router_w, router_bias, exp_w_gu, exp_w_down, shared_w_gu, shared_w_down, final_norm):
    batch, seq, d = x.shape
    n = batch * seq
    depth = mod_w.shape[0]
    heads = RET_HEADS
    qk_cols = d
    mods = _modulation(c, mod_w, mod_b)
    xt = x.reshape(n, d)
    for i in range(depth):
        modb = _mod_block(mods[i], d)
        g_mix = norm_mix[i].reshape(1, d)
        g_ffn = norm_ffn[i].reshape(1, d)
        j = i // 2
        if i % 2 == 0:
            w_in = ret_w_in[j]
            w_in = jnp.concatenate([_pairs_to_halves(w_in[:, :qk_cols], heads),
                                    _pairs_to_halves(w_in[:, qk_cols:2 * qk_cols], heads),
                                    w_in[:, 2 * qk_cols:]], axis=1).astype(BF16)
            proj = _normmod_proj(xt, g_mix, modb, w_in, seq, glu=False)
            y = _retention_core(proj, positions, ret_out_gain[j], batch, seq, d)
            w_out = ret_w_out[j].astype(BF16)
        else:
            u = _normmod_proj(xt, g_mix, modb, conv_w_in[j].astype(BF16), seq, glu=True)
            y = _conv_ln_silu(u, conv_dw_w[j], conv_dw_b[j], conv_ln_g[j], conv_ln_b[j], batch, seq)
            w_out = conv_w_out[j].astype(BF16)
        router_wt = _expert_major_rows(router_w[i].T)
        x1, h2p, logits_t = _out_projection(y, w_out, xt, modb, g_ffn, router_wt, seq)
        xt = _moe(x1, h2p, logits_t, modb, _expert_major_rows(router_bias[i]),
                  exp_w_gu, exp_w_down, i,
                  shared_w_gu[i].astype(BF16), shared_w_down[i].astype(BF16),
                  final_norm, seq, final=(i == depth - 1))
    return xt.reshape(batch, seq, d)
```

```python
import functools

import jax
import jax.numpy as jnp
from jax import lax
from jax.experimental import pallas as pl
from jax.experimental.pallas import tpu as pltpu

F32 = jnp.float32
BF16 = jnp.bfloat16
I32 = jnp.int32
U32 = jnp.uint32

RET_HEADS = 8
RET_CHUNK = 256
ROPE_BASE = 10000.0
CONV_WIDTH = 31
CONV_HALO = 32
CONV_ACC_VREGS = 32
N_GROUPS = 8
TOPK_GROUPS = 4
TOP_K = 8
ROUTED_SCALE = 2.5
NORM_EPS = 1e-6
MOD_ROWS = 8
LANES = 128
SUBLANES = 8
ROW_SUB = 8
COMBINE_GROUP = 16
DISPATCH_SLICES = 4
DMA_THREADS = 2
EXPERT_BLOCK_ROWS = 512
VMEM_LIMIT = 56 * 1024 * 1024


def _cparams(*sem):
    return pltpu.CompilerParams(dimension_semantics=sem, vmem_limit_bytes=VMEM_LIMIT)


def _tile(n, pref):
    t = min(n, pref)
    assert n % t == 0, (n, pref)
    return t


def _split_bf16(a):
    hi = a.astype(BF16)
    lo = (a - hi.astype(F32)).astype(BF16)
    return hi, lo


def _dot3(a, b, dims):
    ah, al = _split_bf16(a)
    bh, bl = _split_bf16(b)
    dg = functools.partial(lax.dot_general, dimension_numbers=dims, preferred_element_type=F32)
    return dg(ah, bh) + dg(ah, bl) + dg(al, bh)


_NN = (((1,), (0,)), ((), ()))
_NT = (((1,), (1,)), ((), ()))
_TN = (((0,), (0,)), ((), ()))


def _normmod(x, g, shift, scale):
    y = x * lax.rsqrt(jnp.mean(x * x, axis=-1, keepdims=True) + NORM_EPS)
    return (y * g) * (1.0 + scale) + shift


def _silu(x):
    return x * jax.nn.sigmoid(x)


def _pack_rows(h):
    half = h.shape[1] // 2
    bits = lax.bitcast_convert_type(h.astype(BF16).astype(F32), U32)
    lo = lax.shift_right_logical(bits[:, :half], jnp.uint32(16))
    hi = bits[:, half:] & jnp.uint32(0xFFFF0000)
    return hi | lo


def _unpack_rows(w):
    lo = lax.bitcast_convert_type(lax.shift_left(w, jnp.uint32(16)), F32)
    hi = lax.bitcast_convert_type(w & jnp.uint32(0xFFFF0000), F32)
    return lo, hi


def _store_row_tiles(ref, base, packed):
    t = packed.shape[0]
    assert packed.shape[1] == ROW_SUB * LANES
    for s in range(ROW_SUB):
        ref[pl.ds(base + s, t, stride=ROW_SUB), :] = packed[:, s * LANES:(s + 1) * LANES]


def _load_row_tiles(ref, base, t):
    parts = [_unpack_rows(ref[pl.ds(base + s, t, stride=ROW_SUB), :]) for s in range(ROW_SUB)]
    lo = jnp.concatenate([p[0] for p in parts], axis=1)
    hi = jnp.concatenate([p[1] for p in parts], axis=1)
    return lo, hi


def _load_row_tiles_bf16(ref, base, t):
    lo, hi = _load_row_tiles(ref, base, t)
    return jnp.concatenate([lo.astype(BF16), hi.astype(BF16)], axis=1)


def _mod_kernel(c_ref, w_ref, b_ref, o_ref):
    c = c_ref[...]
    o_ref[0] = _dot3(_silu(c), w_ref[0], _NN) + b_ref[0]


def _modulation(c, mod_w, mod_b):
    depth, d, n6 = mod_w.shape
    b = c.shape[0]
    assert b <= MOD_ROWS
    c_pad = jnp.zeros((MOD_ROWS, d), F32).at[:b].set(c)
    tn = _tile(n6, 1024)
    out = pl.pallas_call(
        _mod_kernel,
        grid=(depth, n6 // tn),
        in_specs=[pl.BlockSpec((MOD_ROWS, d), lambda i, j: (0, 0)),
                  pl.BlockSpec((1, d, tn), lambda i, j: (i, 0, j)),
                  pl.BlockSpec((1, 1, tn), lambda i, j: (i, 0, j))],
        out_specs=pl.BlockSpec((1, MOD_ROWS, tn), lambda i, j: (i, 0, j)),
        out_shape=jax.ShapeDtypeStruct((depth, MOD_ROWS, n6), F32),
        compiler_params=_cparams("arbitrary", "arbitrary"),
        name="adaln_modulation",
    )(c_pad, mod_w, mod_b.reshape(depth, 1, n6))
    return out[:, :b]


def _mod_block(mod_i, d):
    b = mod_i.shape[0]
    m = mod_i.reshape(b, 6, d)
    return jnp.concatenate([m, jnp.zeros((b, MOD_ROWS - 6, d), F32)], axis=1)


def _proj_kernel(x_ref, g_ref, mod_ref, w_ref, o_ref, h_scr):
    @pl.when(pl.program_id(1) == 0)
    def _():
        m = mod_ref[0]
        h_scr[...] = _normmod(x_ref[...], g_ref[...], m[0:1], m[1:2]).astype(BF16)

    o_ref[...] = jnp.dot(h_scr[...], w_ref[...], preferred_element_type=F32).astype(o_ref.dtype)


def _glu_proj_kernel(x_ref, g_ref, mod_ref, wa_ref, wb_ref, o_ref, h_scr):
    @pl.when(pl.program_id(1) == 0)
    def _():
        m = mod_ref[0]
        h_scr[...] = _normmod(x_ref[...], g_ref[...], m[0:1], m[1:2]).astype(BF16)

    h = h_scr[...]
    a = jnp.dot(h, wa_ref[...], preferred_element_type=F32)
    b = jnp.dot(h, wb_ref[...], preferred_element_type=F32)
    o_ref[...] = a * jax.nn.sigmoid(b)


def _normmod_proj(x, g, modb, w, seq, glu):
    n, d = x.shape
    nout = w.shape[1] // 2 if glu else w.shape[1]
    tm = _tile(seq, 1024)
    tn = _tile(nout, 512 if glu else 1024)
    tiles_per_seq = seq // tm
    x_spec = pl.BlockSpec((tm, d), lambda i, j: (i, 0))
    g_spec = pl.BlockSpec((1, d), lambda i, j: (0, 0))
    m_spec = pl.BlockSpec((1, MOD_ROWS, d), lambda i, j: (i // tiles_per_seq, 0, 0))
    if glu:
        half_blocks = nout // tn
        in_specs = [x_spec, g_spec, m_spec,
                    pl.BlockSpec((d, tn), lambda i, j: (0, j)),
                    pl.BlockSpec((d, tn), lambda i, j: (0, j + half_blocks))]
        body, args, odt = _glu_proj_kernel, (x, g, modb, w, w), F32
    else:
        in_specs = [x_spec, g_spec, m_spec, pl.BlockSpec((d, tn), lambda i, j: (0, j))]
        body, args, odt = _proj_kernel, (x, g, modb, w), BF16
    return pl.pallas_call(
        body,
        grid=(n // tm, nout // tn),
        in_specs=in_specs,
        out_specs=pl.BlockSpec((tm, tn), lambda i, j: (i, j)),
        out_shape=jax.ShapeDtypeStruct((n, nout), odt),
        scratch_shapes=[pltpu.VMEM((tm, d), BF16)],
        compiler_params=_cparams("arbitrary", "arbitrary"),
        name="glu_in_projection" if glu else "in_projection",
    )(*args)


def _retention_kernel(pos_ref, inv_ref, q_ref, k_ref, v_ref, g_ref, intra_ref, qd_ref, kd_ref,
                      cd_ref, gain_ref, o_ref, state, *, dk, dv):
    @pl.when(pl.program_id(1) == 0)
    def _():
        state[...] = jnp.zeros_like(state)

    half = dk // 2
    ang = pos_ref[...].astype(F32) * inv_ref[...]
    cos = jnp.cos(ang)
    sin = jnp.sin(ang)

    def rot(ref, h):
        x1 = ref[:, h * dk:h * dk + half].astype(F32)
        x2 = ref[:, h * dk + half:(h + 1) * dk].astype(F32)
        return jnp.concatenate([x1 * cos - x2 * sin, x1 * sin + x2 * cos], axis=1)

    for h in range(RET_HEADS):
        q = rot(q_ref, h)
        k = rot(k_ref, h) * (dk ** -0.5)
        v = v_ref[:, h * dv:(h + 1) * dv]
        s = lax.dot_general(q.astype(BF16), k.astype(BF16), _NT, preferred_element_type=F32)
        p = (s * intra_ref[h]).astype(BF16)
        inner = jnp.dot(p, v, preferred_element_type=F32)
        st = state[h]
        cross = jnp.dot((q * qd_ref[h]).astype(BF16), st.astype(BF16),
                        preferred_element_type=F32)
        kv = lax.dot_general((k * kd_ref[h]).astype(BF16), v, _TN, preferred_element_type=F32)
        state[h] = cd_ref[h][:, :1] * st + kv
        o = inner + cross
        o = o * lax.rsqrt(jnp.mean(o * o, axis=-1, keepdims=True) + NORM_EPS)
        gate = g_ref[:, h * dv:(h + 1) * dv].astype(F32)
        o_ref[:, h * dv:(h + 1) * dv] = (_silu(gate) * (o * gain_ref[:, h * dv:(h + 1) * dv])
                                         ).astype(o_ref.dtype)


def _retention_core(proj, positions, out_gain, batch, seq, d):
    n = batch * seq
    heads, dk = RET_HEADS, d // RET_HEADS
    dv = 2 * dk
    c = _tile(seq, RET_CHUNK)
    nc = seq // c
    half = dk // 2
    inv = (1.0 / (ROPE_BASE ** jnp.linspace(0.0, 1.0, half, dtype=F32))).reshape(1, half)
    log_gamma = jnp.log1p(-jnp.exp2(-5.0 - jnp.arange(heads, dtype=F32)))
    idx = jnp.arange(c, dtype=F32)
    rel = idx[:, None] - idx[None, :]
    intra = jnp.where(rel >= 0, jnp.exp(log_gamma[:, None, None] * jnp.maximum(rel, 0.0)), 0.0)
    q_decay = jnp.exp(log_gamma[:, None] * (idx + 1.0))[:, :, None]
    k_decay = jnp.exp(log_gamma[:, None] * (c - 1.0 - idx))[:, :, None]
    chunk_decay = jnp.broadcast_to(jnp.exp(log_gamma * c)[:, None, None], (heads, 1, LANES))
    row = lambda b, j: b * nc + j
    full3 = lambda b, j: (0, 0, 0)
    return pl.pallas_call(
        functools.partial(_retention_kernel, dk=dk, dv=dv),
        grid=(batch, nc),
        in_specs=[pl.BlockSpec((c, 1), lambda b, j: (row(b, j), 0)),
                  pl.BlockSpec((1, half), lambda b, j: (0, 0)),
                  pl.BlockSpec((c, heads * dk), lambda b, j: (row(b, j), 0)),
                  pl.BlockSpec((c, heads * dk), lambda b, j: (row(b, j), 1)),
                  pl.BlockSpec((c, heads * dv), lambda b, j: (row(b, j), 1)),
                  pl.BlockSpec((c, heads * dv), lambda b, j: (row(b, j), 2)),
                  pl.BlockSpec((heads, c, c), full3),
                  pl.BlockSpec((heads, c, 1), full3),
                  pl.BlockSpec((heads, c, 1), full3),
                  pl.BlockSpec((heads, 1, LANES), full3),
                  pl.BlockSpec((1, heads * dv), lambda b, j: (0, 0))],
        out_specs=pl.BlockSpec((c, heads * dv), lambda b, j: (row(b, j), 0)),
        out_shape=jax.ShapeDtypeStruct((n, heads * dv), BF16),
        scratch_shapes=[pltpu.VMEM((heads, dk, dv), F32)],
        compiler_params=_cparams("arbitrary", "arbitrary"),
        name="retention_core",
    )(positions.reshape(n, 1), inv, proj, proj, proj, proj, intra, q_decay, k_decay,
      chunk_decay, out_gain.reshape(1, heads * dv))


def _conv_kernel(u_ref, halo_ref, w_ref, b_ref, lg_ref, lb_ref, o_ref, ext, conv, *, tile):
    first = pl.program_id(1) == 0
    ext[0, 0:CONV_HALO, :] = jnp.where(first, 0.0, halo_ref[...])
    ext[0, CONV_HALO:CONV_HALO + tile, :] = u_ref[...]
    span = ext.shape[1] - SUBLANES
    for s in range(1, SUBLANES):
        ext[s, 0:span, :] = ext[0, s:s + span, :]
    base = CONV_HALO - (CONV_WIDTH - 1)
    ch = u_ref.shape[1]
    cw = min(ch, max(LANES, CONV_ACC_VREGS * SUBLANES // tile * LANES))
    for c0 in range(0, ch, cw):
        part = jnp.zeros((tile, cw), F32) + b_ref[:, c0:c0 + cw]
        for j in range(CONV_WIDTH):
            shift, start = (base + j) % SUBLANES, (base + j) // SUBLANES * SUBLANES
            part = part + ext[shift, start:start + tile, c0:c0 + cw] * w_ref[j:j + 1, c0:c0 + cw]
        conv[:, c0:c0 + cw] = part
    acc = conv[...]
    mu = jnp.mean(acc, axis=-1, keepdims=True)
    cen = acc - mu
    var = jnp.mean(cen * cen, axis=-1, keepdims=True)
    y = cen * lax.rsqrt(var + NORM_EPS) * lg_ref[...] + lb_ref[...]
    o_ref[...] = _silu(y).astype(o_ref.dtype)


def _conv_ln_silu(u, dw_w, dw_b, ln_g, ln_b, batch, seq):
    n, ch = u.shape
    t = _tile(seq, 128)
    assert t % CONV_HALO == 0
    nt = seq // t
    halo_per_tile = t // CONV_HALO
    row = lambda b, j: b * nt + j
    vec = pl.BlockSpec((1, ch), lambda b, j: (0, 0))
    return pl.pallas_call(
        functools.partial(_conv_kernel, tile=t),
        grid=(batch, nt),
        in_specs=[pl.BlockSpec((t, ch), lambda b, j: (row(b, j), 0)),
                  pl.BlockSpec((CONV_HALO, ch),
                               lambda b, j: (jnp.maximum(row(b, j) * halo_per_tile - 1, 0), 0)),
                  pl.BlockSpec((CONV_WIDTH, ch), lambda b, j: (0, 0)),
                  vec, vec, vec],
        out_specs=pl.BlockSpec((t, ch), lambda b, j: (row(b, j), 0)),
        out_shape=jax.ShapeDtypeStruct((n, ch), BF16),
        scratch_shapes=[pltpu.VMEM((SUBLANES, CONV_HALO + t, ch), F32),
                        pltpu.VMEM((t, ch), F32)],
        compiler_params=_cparams("arbitrary", "arbitrary"),
        name="conv_ln_silu",
    )(u, u, dw_w, dw_b.reshape(1, ch), ln_g.reshape(1, ch), ln_b.reshape(1, ch))


def _outproj_kernel(y_ref, w_ref, x_ref, mod_ref, g_ref, rw_ref, x1_ref, h2_ref, lg_ref, acc):
    k = pl.program_id(1)

    @pl.when(k == 0)
    def _():
        acc[...] = jnp.zeros_like(acc)

    acc[...] += jnp.dot(y_ref[...], w_ref[...], preferred_element_type=F32)

    @pl.when(k == pl.num_programs(1) - 1)
    def _():
        m = mod_ref[0]
        x1 = x_ref[...] + m[2:3] * acc[...]
        x1_ref[...] = x1
        h2 = _normmod(x1, g_ref[...], m[3:4], m[4:5])
        _store_row_tiles(h2_ref, 0, _pack_rows(h2))
        lg_ref[...] = _dot3(rw_ref[...], h2, _NT)


def _out_projection(y, w, x, modb, g_ffn, router_wt, seq):
    n, kdim = y.shape
    d = x.shape[1]
    ne = router_wt.shape[0]
    assert d // 2 == ROW_SUB * LANES
    tm = _tile(seq, 512)
    tk = _tile(kdim, 1024)
    tiles_per_seq = seq // tm
    return pl.pallas_call(
        _outproj_kernel,
        grid=(n // tm, kdim // tk),
        in_specs=[pl.BlockSpec((tm, tk), lambda i, k: (i, k)),
                  pl.BlockSpec((tk, d), lambda i, k: (k, 0)),
                  pl.BlockSpec((tm, d), lambda i, k: (i, 0)),
                  pl.BlockSpec((1, MOD_ROWS, d), lambda i, k: (i // tiles_per_seq, 0, 0)),
                  pl.BlockSpec((1, d), lambda i, k: (0, 0)),
                  pl.BlockSpec((ne, d), lambda i, k: (0, 0))],
        out_specs=[pl.BlockSpec((tm, d), lambda i, k: (i, 0)),
                   pl.BlockSpec((tm * ROW_SUB, LANES), lambda i, k: (i, 0)),
                   pl.BlockSpec((ne, tm), lambda i, k: (0, i))],
        out_shape=[jax.ShapeDtypeStruct((n, d), F32),
                   jax.ShapeDtypeStruct((n * ROW_SUB, LANES), U32),
                   jax.ShapeDtypeStruct((ne, n), F32)],
        scratch_shapes=[pltpu.VMEM((tm, d), F32)],
        compiler_params=_cparams("arbitrary", "arbitrary"),
        name="out_projection",
    )(y, w, x, modb, g_ffn, router_wt)


def _routing_kernel(lg_ref, bias_ref, eidx_ref, wts_ref):
    neg = -jnp.inf
    per = N_GROUPS
    tr = lg_ref.shape[1]
    scores = jax.nn.sigmoid(lg_ref[...])
    biased = scores + bias_ref[...]
    n_slab = lg_ref.shape[0] // per
    s_j = [scores[per * j:per * (j + 1), :] for j in range(n_slab)]
    b_j = [biased[per * j:per * (j + 1), :] for j in range(n_slab)]

    m1 = functools.reduce(jnp.maximum, b_j)
    ties = functools.reduce(lambda a, b: a + b, [(b == m1).astype(F32) for b in b_j])
    below = functools.reduce(jnp.maximum, [jnp.where(b < m1, b, neg) for b in b_j])
    grp = m1 + jnp.where(ties >= 2.0, m1, below)

    gid = lax.broadcasted_iota(I32, (per, tr), 0).astype(F32)
    chosen = jnp.zeros((per, tr), F32)
    cur = grp
    for _ in range(TOPK_GROUPS):
        mx = jnp.max(cur, axis=0, keepdims=True)
        first = jnp.min(jnp.where(cur == mx, gid, float(per)), axis=0, keepdims=True)
        pick = gid == first
        chosen = jnp.where(pick, 1.0, chosen)
        cur = jnp.where(pick, neg, cur)

    eid_j = [gid * float(n_slab) + float(j) for j in range(n_slab)]
    cur_j = [jnp.where(chosen > 0.0, b, neg) for b in b_j]
    picked_scores = []
    for k in range(TOP_K):
        mx = jnp.max(functools.reduce(jnp.maximum, cur_j), axis=0, keepdims=True)
        cand = functools.reduce(
            jnp.minimum, [jnp.where(c == mx, e, float(per * n_slab)) for c, e in zip(cur_j, eid_j)])
        first = jnp.min(cand, axis=0, keepdims=True)
        pick_j = [e == first for e in eid_j]
        sc = functools.reduce(lambda a, b: a + b,
                              [jnp.where(p, s, 0.0) for p, s in zip(pick_j, s_j)])
        picked_scores.append(jnp.sum(sc, axis=0, keepdims=True))
        cur_j = [jnp.where(p, neg, c) for p, c in zip(pick_j, cur_j)]
        eidx_ref[k:k + 1, :] = first.astype(I32)
    total = functools.reduce(lambda a, b: a + b, picked_scores)
    for k in range(TOP_K):
        wts_ref[k:k + 1, :] = picked_scores[k] / total * ROUTED_SCALE


def _routing(logits_t, bias_perm):
    ne, n = logits_t.shape
    tr = _tile(n, 1024)
    return pl.pallas_call(
        _routing_kernel,
        grid=(n // tr,),
        in_specs=[pl.BlockSpec((ne, tr), lambda i: (0, i)),
                  pl.BlockSpec((ne, 1), lambda i: (0, 0))],
        out_specs=[pl.BlockSpec((TOP_K, tr), lambda i: (0, i)),
                   pl.BlockSpec((TOP_K, tr), lambda i: (0, i))],
        out_shape=[jax.ShapeDtypeStruct((TOP_K, n), I32),
                   jax.ShapeDtypeStruct((TOP_K, n), F32)],
        compiler_params=_cparams("arbitrary"),
        name="moe_routing",
    )(logits_t, bias_perm.reshape(ne, 1))


def _plan_kernel(e_ref, u_ref, l_ref, dest_ref, be_ref, nu_ref, ss_ref, sn_ref, sc_ref, cnt, base,
                 carry, *, tb, ne):
    p = pl.program_id(0)
    t = pl.program_id(1)
    tp = e_ref.shape[1]
    ei = e_ref[...]
    eid = lax.broadcasted_iota(I32, (ne, tp), 0)
    hit = [ei[k:k + 1, :] == eid for k in range(TOP_K)]
    onehot = functools.reduce(lambda a, b: a + b, [h.astype(F32) for h in hit])
    tile_cnt = jnp.sum(onehot, axis=1, keepdims=True)

    @pl.when((p == 0) & (t == 0))
    def _():
        cnt[...] = jnp.zeros_like(cnt)

    @pl.when(p == 0)
    def _():
        cnt[...] += tile_cnt

    @pl.when((p == 1) & (t == 0))
    def _():
        nblk = jnp.floor((cnt[...] + float(tb - 1)) * (1.0 / tb))
        start_blk = jnp.dot(l_ref[...], nblk.astype(BF16), preferred_element_type=F32)
        base[...] = start_blk * float(tb)
        carry[...] = jnp.zeros_like(carry)
        end_blk = start_blk + nblk
        nbp = be_ref.shape[1]
        blk = lax.broadcasted_iota(I32, (ne, nbp), 1).astype(F32)
        owner = jnp.sum((end_blk[:, :1] <= blk).astype(F32), axis=0, keepdims=True)
        be_ref[...] = jnp.minimum(owner, float(ne - 1)).astype(I32)
        nu_ref[...] = end_blk[ne - 1:ne, :].astype(I32)
        diag = (lax.broadcasted_iota(I32, (ne, LANES), 0)
                == lax.broadcasted_iota(I32, (ne, LANES), 1))
        ss_ref[...] = jnp.sum(jnp.where(diag, start_blk, 0.0), axis=0, keepdims=True).astype(I32)
        sn_ref[...] = jnp.sum(jnp.where(diag, nblk, 0.0), axis=0, keepdims=True).astype(I32)
        sc_ref[...] = jnp.sum(jnp.where(diag, cnt[...], 0.0), axis=0, keepdims=True).astype(I32)

    @pl.when(p == 1)
    def _():
        before = jnp.dot(onehot.astype(BF16), u_ref[...], preferred_element_type=F32)
        rowpos = base[:, :1] + carry[:, :1] + before
        tok = dest_ref.shape[2]
        for k in range(TOP_K):
            row = jnp.sum(jnp.where(hit[k], rowpos, 0.0), axis=0, keepdims=True).astype(I32)
            for c in range(dest_ref.shape[0]):
                dest_ref[c, k:k + 1, :] = row[:, c * tok:(c + 1) * tok]
        carry[...] += tile_cnt


def _dispatch_plan(eidx, ne, tb, n_blocks, tok_tile):
    n = eidx.shape[1]
    tp = _tile(n, 2 * tok_tile)
    assert n // tb + 1 <= 256
    assert ne <= LANES
    lane_row = pl.BlockSpec((1, LANES), lambda p, t: (0, 0))
    nbp = -(-n_blocks // LANES) * LANES
    upper = (jnp.arange(tp)[:, None] < jnp.arange(tp)[None, :]).astype(BF16)
    lower = (jnp.arange(ne)[None, :] < jnp.arange(ne)[:, None]).astype(BF16)
    return pl.pallas_call(
        functools.partial(_plan_kernel, tb=tb, ne=ne),
        grid=(2, n // tp),
        in_specs=[pl.BlockSpec((TOP_K, tp), lambda p, t: (0, t)),
                  pl.BlockSpec((tp, tp), lambda p, t: (0, 0)),
                  pl.BlockSpec((ne, ne), lambda p, t: (0, 0))],
        out_specs=[pl.BlockSpec((tp // tok_tile, TOP_K, tok_tile), lambda p, t: (t * p, 0, 0)),
                   pl.BlockSpec((1, nbp), lambda p, t: (0, 0)),
                   lane_row, lane_row, lane_row, lane_row],
        out_shape=[jax.ShapeDtypeStruct((n // tok_tile, TOP_K, tok_tile), I32),
                   jax.ShapeDtypeStruct((1, nbp), I32)]
        + [jax.ShapeDtypeStruct((1, LANES), I32)] * 4,
        scratch_shapes=[pltpu.VMEM((ne, LANES), F32)] * 3,
        compiler_params=_cparams("arbitrary", "arbitrary"),
        name="moe_dispatch_plan",
    )(eidx, upper, lower)


def _rows(ref, row, count=1):
    return ref.at[pl.ds(pl.multiple_of(row * ROW_SUB, ROW_SUB), count * ROW_SUB), :]


def _row_copy(src, s_row, dst, d_row, sem):
    return pltpu.make_async_copy(_rows(src, s_row), _rows(dst, d_row), sem)


def _dispatch_kernel(ss_ref, sn_ref, dest_ref, h_ref, sgu_ref, sdn_ref, xs_hbm, sh_ref, zeros, sem,
                     zsem, *, td, tb, ne):
    @pl.when(pl.program_id(0) == 0)
    def _():
        zeros[...] = jnp.zeros_like(zeros)

        def fill_copy(blk):
            return pltpu.make_async_copy(zeros, _rows(xs_hbm, blk * tb, tb), zsem)

        def fill(e, carry):
            @pl.when(sn_ref[e] > 0)
            def _():
                fill_copy(ss_ref[e] + sn_ref[e] - 1).start()
            return carry

        def filled(e, carry):
            @pl.when(sn_ref[e] > 0)
            def _():
                fill_copy(0).wait()
            return carry

        lax.fori_loop(0, ne, fill, 0)
        lax.fori_loop(0, ne, filled, 0)

    def issue(t, carry):
        for k in range(TOP_K):
            _row_copy(h_ref, t, xs_hbm, dest_ref[0, k, t], sem).start(priority=k % DMA_THREADS)
        return carry

    ff = sdn_ref.shape[0]
    d = sgu_ref.shape[0]
    per_phase = td // (2 * DISPATCH_SLICES)
    kc, nc = d // DISPATCH_SLICES, d // DISPATCH_SLICES
    x = _load_row_tiles_bf16(h_ref, 0, td)
    gu = None
    for c in range(DISPATCH_SLICES):
        lax.fori_loop(c * per_phase, (c + 1) * per_phase, issue, 0)
        part = jnp.dot(x[:, c * kc:(c + 1) * kc], sgu_ref[c * kc:(c + 1) * kc, :],
                       preferred_element_type=F32)
        gu = part if c == 0 else gu + part
    act = (_silu(gu[:, :ff]) * gu[:, ff:]).astype(BF16)
    outs = []
    for c in range(DISPATCH_SLICES):
        p0 = (DISPATCH_SLICES + c) * per_phase
        lax.fori_loop(p0, p0 + per_phase, issue, 0)
        outs.append(jnp.dot(act, sdn_ref[:, c * nc:(c + 1) * nc], preferred_element_type=F32))
    _store_row_tiles(sh_ref, 0, _pack_rows(jnp.concatenate(outs, axis=1)))

    def drain(t, carry):
        for _ in range(TOP_K):
            _row_copy(h_ref, 0, xs_hbm, 0, sem).wait()
        return carry

    lax.fori_loop(0, td, drain, 0)


def _dispatch(dest3, h2p, seg_start, seg_nblk, s_gu, s_dn, n_rows, tb, ne):
    nt, _, td = dest3.shape
    tile = pl.BlockSpec((td * ROW_SUB, LANES), lambda i, ss, sn: (i, 0))
    grid_spec = pltpu.PrefetchScalarGridSpec(
        num_scalar_prefetch=2,
        grid=(nt,),
        in_specs=[pl.BlockSpec((1, TOP_K, td), lambda i, ss, sn: (i, 0, 0),
                               memory_space=pltpu.SMEM),
                  tile,
                  pl.BlockSpec(s_gu.shape, lambda i, ss, sn: (0, 0)),
                  pl.BlockSpec(s_dn.shape, lambda i, ss, sn: (0, 0))],
        out_specs=[pl.BlockSpec(memory_space=pl.ANY), tile],
        scratch_shapes=[pltpu.VMEM((tb * ROW_SUB, LANES), U32),
                        pltpu.SemaphoreType.DMA(()), pltpu.SemaphoreType.DMA(())])
    return pl.pallas_call(
        functools.partial(_dispatch_kernel, td=td, tb=tb, ne=ne),
        grid_spec=grid_spec,
        out_shape=[jax.ShapeDtypeStruct((n_rows * ROW_SUB, LANES), U32),
                   jax.ShapeDtypeStruct(h2p.shape, U32)],
        compiler_params=_cparams("arbitrary"),
        name="moe_row_dispatch",
    )(seg_start, seg_nblk, dest3, h2p, s_gu, s_dn)


def _expert_kernel(be_ref, nu_ref, sn_ref, ss_ref, sc_ref, x_ref, wgu_hbm, wdn_hbm, o_ref, gu_f32,
                   dn_f32, wgu_b, wdn_b, sem, *, layer, ne):
    i = pl.program_id(0)
    live = i < nu_ref[0]
    e = be_ref[i]
    new_expert = (i == 0) | (e != be_ref[jnp.maximum(i - 1, 0)])

    def weight_copies(expert):
        return (pltpu.make_async_copy(wgu_hbm.at[layer, expert], gu_f32, sem.at[0]),
                pltpu.make_async_copy(wdn_hbm.at[layer, expert], dn_f32, sem.at[1]))

    @pl.when(live & (i == 0))
    def _():
        for cp in weight_copies(e):
            cp.start()

    @pl.when(live & new_expert)
    def _():
        for cp in weight_copies(e):
            cp.wait()
        wgu_b[...] = gu_f32[...].astype(BF16)
        wdn_b[...] = dn_f32[...].astype(BF16)
        nxt = lax.while_loop(lambda j: (j < ne) & (sn_ref[jnp.minimum(j, ne - 1)] == 0),
                             lambda j: j + 1, e + 1)

        @pl.when(nxt < ne)
        def _():
            for cp in weight_copies(nxt):
                cp.start()

    tb = x_ref.shape[0] // ROW_SUB
    half = tb // 2
    rows = sc_ref[e] - (i - ss_ref[e]) * tb

    def swiglu_rows(n_rows):
        ff = wdn_b.shape[0]
        x = _load_row_tiles_bf16(x_ref, 0, n_rows)
        gu = jnp.dot(x, wgu_b[...], preferred_element_type=F32)
        act = (_silu(gu[:, :ff]) * gu[:, ff:]).astype(BF16)
        _store_row_tiles(o_ref, 0, _pack_rows(jnp.dot(act, wdn_b[...], preferred_element_type=F32)))

    @pl.when(live & (rows > half))
    def _():
        swiglu_rows(tb)

    @pl.when(live & (rows <= half))
    def _():
        swiglu_rows(half)
        o_ref[half * ROW_SUB:, :] = jnp.zeros((half * ROW_SUB, LANES), U32)


def _expert_ffn(xs, block_e, n_used, seg_nblk, seg_start, seg_cnt, w_gu, w_dn, layer, tb):
    _, ne, d, ff2 = w_gu.shape
    ff = w_dn.shape[2]
    nb = xs.shape[0] // (tb * ROW_SUB)
    row_block = pl.BlockSpec((tb * ROW_SUB, LANES),
                             lambda i, be, nu, sn, ss, sc: (jnp.minimum(i, nu[0] - 1), 0))
    grid_spec = pltpu.PrefetchScalarGridSpec(
        num_scalar_prefetch=5,
        grid=(nb,),
        in_specs=[row_block, pl.BlockSpec(memory_space=pl.ANY), pl.BlockSpec(memory_space=pl.ANY)],
        out_specs=row_block,
        scratch_shapes=[pltpu.VMEM((d, ff2), F32), pltpu.VMEM((ff, d), F32),
                        pltpu.VMEM((d, ff2), BF16), pltpu.VMEM((ff, d), BF16),
                        pltpu.SemaphoreType.DMA((2,))])
    return pl.pallas_call(
        functools.partial(_expert_kernel, layer=layer, ne=ne),
        grid_spec=grid_spec,
        out_shape=jax.ShapeDtypeStruct(xs.shape, U32),
        compiler_params=_cparams("arbitrary"),
        name="moe_expert_ffn",
    )(block_e, n_used, seg_nblk, seg_start, seg_cnt, xs, w_gu, w_dn)


def _combine_kernel(dest_ref, next_ref, wts_ref, ys_hbm, sh_ref, x_ref, mod_ref, fn_ref, o_ref, gbuf,
                    sem, *, tc, final):
    i = pl.program_id(0)
    last = i == pl.num_programs(0) - 1
    slot_rows = TOP_K * tc
    slot = i % 2
    other = 1 - slot

    def start_rows(d_ref, t, to_slot):
        for k in range(TOP_K):
            _row_copy(ys_hbm, d_ref[0, k, t], gbuf, to_slot * slot_rows + k * tc + t,
                      sem.at[to_slot]).start(priority=k % DMA_THREADS)

    def drain(of_slot):
        def body(t, carry):
            for _ in range(TOP_K):
                _row_copy(ys_hbm, 0, gbuf, 0, sem.at[of_slot]).wait()
            return carry

        lax.fori_loop(0, tc, body, 0)

    @pl.when(i == 0)
    def _():
        def body(t, carry):
            start_rows(dest_ref, t, 0)
            return carry

        lax.fori_loop(0, tc, body, 0)

    drain(slot)

    base = slot * slot_rows * ROW_SUB
    gate = mod_ref[0][5:6]

    def group(rg, carry):
        r0 = pl.multiple_of(rg * COMBINE_GROUP, COMBINE_GROUP)
        w = wts_ref[pl.ds(r0, COMBINE_GROUP), :]
        per_k = COMBINE_GROUP // TOP_K
        acc_lo = acc_hi = None
        for k in range(TOP_K):
            for t in range(k * per_k, (k + 1) * per_k):
                start_rows(next_ref, r0 + t, other)
            lo, hi = _load_row_tiles(gbuf, base + (k * tc + r0) * ROW_SUB, COMBINE_GROUP)
            acc_lo = w[:, k:k + 1] * lo if k == 0 else acc_lo + w[:, k:k + 1] * lo
            acc_hi = w[:, k:k + 1] * hi if k == 0 else acc_hi + w[:, k:k + 1] * hi
        lo, hi = _load_row_tiles(sh_ref, r0 * ROW_SUB, COMBINE_GROUP)
        moe = jnp.concatenate([acc_lo + lo, acc_hi + hi], axis=1)
        o_ref[pl.ds(r0, COMBINE_GROUP), :] = x_ref[pl.ds(r0, COMBINE_GROUP), :] + gate * moe
        return carry

    lax.fori_loop(0, tc // COMBINE_GROUP, group, 0)
    if final:
        x2 = o_ref[...]
        o_ref[...] = (x2 * lax.rsqrt(jnp.mean(x2 * x2, axis=-1, keepdims=True) + NORM_EPS)
                      * fn_ref[...])

    @pl.when(last)
    def _():
        drain(other)


def _combine(dest3, wts_t, ys, shared_rows, x1, modb, final_norm, seq, final):
    n, d = x1.shape
    nt, _, tc = dest3.shape
    tiles_per_seq = seq // tc
    full2 = lambda i: (0, 0)
    return pl.pallas_call(
        functools.partial(_combine_kernel, tc=tc, final=final),
        grid=(nt,),
        in_specs=[pl.BlockSpec((1, TOP_K, tc), lambda i: (i, 0, 0), memory_space=pltpu.SMEM),
                  pl.BlockSpec((1, TOP_K, tc), lambda i: (jnp.minimum(i + 1, nt - 1), 0, 0),
                               memory_space=pltpu.SMEM),
                  pl.BlockSpec((tc, TOP_K), lambda i: (i, 0)),
                  pl.BlockSpec(memory_space=pl.ANY),
                  pl.BlockSpec((tc * ROW_SUB, LANES), lambda i: (i, 0)),
                  pl.BlockSpec((tc, d), lambda i: (i, 0)),
                  pl.BlockSpec((1, MOD_ROWS, d), lambda i: (i // tiles_per_seq, 0, 0)),
                  pl.BlockSpec((1, d), full2)],
        out_specs=pl.BlockSpec((tc, d), lambda i: (i, 0)),
        out_shape=jax.ShapeDtypeStruct((n, d), F32),
        scratch_shapes=[pltpu.VMEM((2 * TOP_K * tc * ROW_SUB, LANES), U32),
                        pltpu.SemaphoreType.DMA((2,))],
        compiler_params=_cparams("arbitrary"),
        name="moe_combine",
    )(dest3, dest3, wts_t, ys, shared_rows, x1, modb, final_norm.reshape(1, d))


def _moe(x1, h2p, logits_t, modb, bias_perm, w_gu, w_dn, layer, s_gu, s_dn, final_norm, seq,
         final):
    n, d = x1.shape
    ne = w_gu.shape[1]
    tb = EXPERT_BLOCK_ROWS
    tok_tile = _tile(seq, 256)
    n_blocks = n * TOP_K // tb + ne
    eidx, wts = _routing(logits_t, bias_perm)
    dest3, block_e, n_used, seg_start, seg_nblk, seg_cnt = _dispatch_plan(eidx, ne, tb, n_blocks,
                                                                          tok_tile)
    seg_start, seg_nblk, seg_cnt = seg_start[0, :ne], seg_nblk[0, :ne], seg_cnt[0, :ne]
    xs, shared_rows = _dispatch(dest3, h2p, seg_start, seg_nblk, s_gu, s_dn, n_blocks * tb, tb, ne)
    ys = _expert_ffn(xs, block_e[0, :n_blocks], n_used[0, :1], seg_nblk, seg_start, seg_cnt,
                     w_gu, w_dn, layer, tb)
    return _combine(dest3, wts.T, ys, shared_rows, x1, modb, final_norm, seq, final)


def _expert_major_rows(a):
    per = a.shape[0] // N_GROUPS
    return a.reshape((N_GROUPS, per) + a.shape[1:]).swapaxes(0, 1).reshape(a.shape)


def _pairs_to_halves(w, heads):
    d, cols = w.shape
    dk = cols // heads
    return w.reshape(d, heads, dk // 2, 2).transpose(0, 1, 3, 2).reshape(d, cols)


def kernel(x, c, positions, mod_w, mod_b, norm_mix, norm_ffn, ret_w_in, ret_w_out, ret_out_gain, conv_w_in, conv_dw_w, conv_dw_b, conv_ln_g, conv_ln_b, conv_w_out, router_w, router_bias, exp_w_gu, exp_w_down, shared_w_gu, shared_w_down, final_norm):
    batch, seq, d = x.shape
    n = batch * seq
    depth = mod_w.shape[0]
    heads = RET_HEADS
    qk_cols = d
    mods = _modulation(c, mod_w, mod_b)
    xt = x.reshape(n, d)
    for i in range(depth):
        modb = _mod_block(mods[i], d)
        g_mix = norm_mix[i].reshape(1, d)
        g_ffn = norm_ffn[i].reshape(1, d)
        j = i // 2
        if i % 2 == 0:
            w_in = ret_w_in[j]
            w_in = jnp.concatenate([_pairs_to_halves(w_in[:, :qk_cols], heads),
                                    _pairs_to_halves(w_in[:, qk_cols:2 * qk_cols], heads),
                                    w_in[:, 2 * qk_cols:]], axis=1).astype(BF16)
            proj = _normmod_proj(xt, g_mix, modb, w_in, seq, glu=False)
            y = _retention_core(proj, positions, ret_out_gain[j], batch, seq, d)
            w_out = ret_w_out[j].astype(BF16)
        else:
            u = _normmod_proj(xt, g_mix, modb, conv_w_in[j].astype(BF16), seq, glu=True)
            y = _conv_ln_silu(u, conv_dw_w[j], conv_dw_b[j], conv_ln_g[j], conv_ln_b[j], batch, seq)
            w_out = conv_w_out[j].astype(BF16)
        router_wt = _expert_major_rows(router_w[i].T)
        x1, h2p, logits_t = _out_projection(y, w_out, xt, modb, g_ffn, router_wt, seq)
        xt = _moe(x1, h2p, logits_t, modb, _expert_major_rows(router_bias[i]),
                  exp_w_gu, exp_w_down, i,
                  shared_w_gu[i].astype(BF16), shared_w_down[i].astype(BF16),
                  final_norm, seq, final=(i == depth - 1))
    return xt.reshape(batch, seq, d)
```
